```python
import math
import jax, jax.numpy as jnp
from jax import lax
import numpy as np

D_MODEL = 1024
BATCH = 4
SEQ = 4096
DEPTH = 2

HEAD_DIM = 64
HEADS_A = 8
HEADS_B = 8
HEADS_C = 8
WIDTH = HEADS_A * HEAD_DIM
N_BRANCH = 3
ROPE_DIM = HEAD_DIM // 4
ROPE_THETA = 500000.0
MOBA_BLOCK = 256
MOBA_TOPK = 3
MOBA_Q_CHUNK = 32
IDX_HEADS = 8
IDX_DIM = 64
DSA_TOPK_MAX = 256
Q_BLOCK = 128
RMS_EPS = 1e-6

SECTION_SIZES = (
    WIDTH, WIDTH, WIDTH, WIDTH,
    WIDTH, WIDTH, WIDTH, WIDTH,
    WIDTH, WIDTH, WIDTH, WIDTH,
    IDX_HEADS * IDX_DIM, IDX_DIM, IDX_HEADS,
    HEADS_C,
    N_BRANCH * D_MODEL,
)
D_IN = sum(SECTION_SIZES)
SPLIT_POINTS = tuple(int(v) for v in np.cumsum(SECTION_SIZES)[:-1])

kernel_name = "hybrid_moba_dsa_fox_gated_trunk"


def rms_norm(x, gain):
    xf = x.astype(jnp.float32)
    y = xf * lax.rsqrt(jnp.mean(xf * xf, axis=-1, keepdims=True) + RMS_EPS)
    return (y * gain.astype(jnp.float32)).astype(x.dtype)


def rope_partial(t, pos):
    half = ROPE_DIM // 2
    inv_freq = jnp.power(jnp.float32(ROPE_THETA), -jnp.arange(0, ROPE_DIM, 2, dtype=jnp.float32) / ROPE_DIM)
    ang = pos.astype(jnp.float32)[:, None] * inv_freq[None, :]
    cos = jnp.cos(ang)[None, :, None, :]
    sin = jnp.sin(ang)[None, :, None, :]
    tr = t[..., :ROPE_DIM].astype(jnp.float32)
    t1, t2 = tr[..., :half], tr[..., half:]
    rot = jnp.concatenate([t1 * cos - t2 * sin, t2 * cos + t1 * sin], axis=-1).astype(t.dtype)
    return jnp.concatenate([rot, t[..., ROPE_DIM:]], axis=-1)


def moba_attention(q, k, v):
    B, S, H, d = q.shape
    n_blk = -(-S // MOBA_BLOCK)
    pad = n_blk * MOBA_BLOCK - S
    kp = jnp.pad(k, ((0, 0), (0, pad), (0, 0), (0, 0)))
    vp = jnp.pad(v, ((0, 0), (0, pad), (0, 0), (0, 0)))
    kb = kp.reshape(B, n_blk, MOBA_BLOCK, H, d).transpose(0, 3, 1, 2, 4)
    vb = vp.reshape(B, n_blk, MOBA_BLOCK, H, d).transpose(0, 3, 1, 2, 4)
    k_mean = jnp.mean(kb, axis=3)
    qh = q.transpose(0, 2, 1, 3)
    k_sel = min(MOBA_TOPK, n_blk - 1)
    scale = d ** -0.5
    bi = jnp.arange(B)[:, None, None, None]
    hi = jnp.arange(H)[None, :, None, None]

    def chunk(ci):
        start = ci * MOBA_Q_CHUNK
        qc = lax.dynamic_slice_in_dim(qh, start, MOBA_Q_CHUNK, axis=2)
        qpos = start + jnp.arange(MOBA_Q_CHUNK)
        own = start // MOBA_BLOCK
        k_own = lax.dynamic_index_in_dim(kb, own, axis=2, keepdims=False)
        v_own = lax.dynamic_index_in_dim(vb, own, axis=2, keepdims=False)
        kpos_own = own * MOBA_BLOCK + jnp.arange(MOBA_BLOCK)
        s_own = jnp.einsum('bhqd,bhkd->bhqk', qc, k_own).astype(jnp.float32) * scale
        s_own = jnp.where(kpos_own[None, :] <= qpos[:, None], s_own, -jnp.inf)
        if k_sel == 0:
            p = jax.nn.softmax(s_own, axis=-1).astype(v.dtype)
            return jnp.einsum('bhqk,bhkd->bhqd', p, v_own)
        gate = jnp.einsum('bhqd,bhnd->bhqn', qc, k_mean).astype(jnp.float32)
        gate = jnp.where((jnp.arange(n_blk) < own)[None, None, None, :], gate, -jnp.inf)
        gval, gidx = lax.top_k(gate, k_sel)
        ok = gval > -jnp.inf
        k_g = kb[bi, hi, gidx]
        v_g = vb[bi, hi, gidx]
        s_past = jnp.einsum('bhqd,bhqnkd->bhqnk', qc, k_g).astype(jnp.float32) * scale
        s_past = jnp.where(ok[..., None], s_past, -jnp.inf)
        s_past = s_past.reshape(B, H, MOBA_Q_CHUNK, k_sel * MOBA_BLOCK)
        p = jax.nn.softmax(jnp.concatenate([s_past, s_own], axis=-1), axis=-1).astype(v.dtype)
        p_past = p[..., :k_sel * MOBA_BLOCK].reshape(B, H, MOBA_Q_CHUNK, k_sel, MOBA_BLOCK)
        p_own = p[..., k_sel * MOBA_BLOCK:]
        return (jnp.einsum('bhqnk,bhqnkd->bhqd', p_past, v_g)
                + jnp.einsum('bhqk,bhkd->bhqd', p_own, v_own))

    out = lax.map(chunk, jnp.arange(S // MOBA_Q_CHUNK))
    return out.transpose(1, 0, 3, 2, 4).reshape(B, S, H, d)


def dsa_attention(q, k, v, q_idx, k_idx, w_idx):
    B, S, H, d = q.shape
    top = min(DSA_TOPK_MAX, S // 4)
    scale = d ** -0.5
    bi = jnp.arange(B)[:, None, None]
    kpos = jnp.arange(S)

    def chunk(ci):
        start = ci * Q_BLOCK
        qpos = start + jnp.arange(Q_BLOCK)
        qi = lax.dynamic_slice_in_dim(q_idx, start, Q_BLOCK, axis=1)
        wi = lax.dynamic_slice_in_dim(w_idx, start, Q_BLOCK, axis=1)
        rel = jax.nn.relu(jnp.einsum('bqhd,bsd->bqhs', qi, k_idx))
        isc = jnp.einsum('bqh,bqhs->bqs', wi, rel).astype(jnp.float32)
        isc = jnp.where(kpos[None, None, :] <= qpos[None, :, None], isc, -jnp.inf)
        ival, idx = lax.top_k(isc, top)
        ok = ival > -jnp.inf
        k_g = k[bi, idx]
        v_g = v[bi, idx]
        qc = lax.dynamic_slice_in_dim(q, start, Q_BLOCK, axis=1)
        s = jnp.einsum('bqhd,bqkhd->bhqk', qc, k_g).astype(jnp.float32) * scale
        s = jnp.where(ok[:, None], s, -jnp.inf)
        p = jax.nn.softmax(s, axis=-1).astype(v.dtype)
        return jnp.einsum('bhqk,bqkhd->bqhd', p, v_g)

    out = lax.map(chunk, jnp.arange(S // Q_BLOCK))
    return out.transpose(1, 0, 2, 3, 4).reshape(B, S, H, d)


def forgetting_attention(q, k, v, log_f):
    B, S, H, d = q.shape
    scale = d ** -0.5
    csum = jnp.cumsum(log_f, axis=1).transpose(0, 2, 1)
    qh = q.transpose(0, 2, 1, 3)
    kh = k.transpose(0, 2, 1, 3)
    vh = v.transpose(0, 2, 1, 3)
    kpos = jnp.arange(S)

    def chunk(ci):
        start = ci * Q_BLOCK
        qpos = start + jnp.arange(Q_BLOCK)
        qc = lax.dynamic_slice_in_dim(qh, start, Q_BLOCK, axis=2)
        cq = lax.dynamic_slice_in_dim(csum, start, Q_BLOCK, axis=2)
        s = jnp.einsum('bhqd,bhkd->bhqk', qc, kh).astype(jnp.float32) * scale
        s = s + (cq[..., :, None] - csum[:, :, None, :])
        s = jnp.where(kpos[None, :] <= qpos[:, None], s, -jnp.inf)
        p = jax.nn.softmax(s, axis=-1).astype(v.dtype)
        return jnp.einsum('bhqk,bhkd->bhqd', p, vh)

    out = lax.map(chunk, jnp.arange(S // Q_BLOCK))
    return out.transpose(1, 0, 3, 2, 4).reshape(B, S, H, d)


def hybrid_layer(x, gain, w_in, f_bias, w_branch, w_out):
    B, S, _ = x.shape
    pos = jnp.arange(S)
    h = rms_norm(x, gain)
    proj = jnp.einsum('bsd,de->bse', h, w_in)
    (qa, ka, va, ga, qb, kb_, vb_, gb, qc, kc, vc, gc,
     q_idx, k_idx, w_idx, f_logit, merge_logit) = jnp.split(proj, SPLIT_POINTS, axis=-1)

    def heads(t, n):
        return t.reshape(B, S, n, HEAD_DIM)

    oa = moba_attention(rope_partial(heads(qa, HEADS_A), pos), rope_partial(heads(ka, HEADS_A), pos),
                        heads(va, HEADS_A)).reshape(B, S, WIDTH) * jax.nn.silu(ga)
    qi = rope_partial(q_idx.reshape(B, S, IDX_HEADS, IDX_DIM), pos)
    ki = rope_partial(k_idx[:, :, None, :], pos)[:, :, 0, :]
    wi = w_idx * (IDX_HEADS ** -0.5 * IDX_DIM ** -0.5)
    ob = dsa_attention(rope_partial(heads(qb, HEADS_B), pos), rope_partial(heads(kb_, HEADS_B), pos),
                       heads(vb_, HEADS_B), qi, ki, wi).reshape(B, S, WIDTH) * jax.nn.silu(gb)
    log_f = jax.nn.log_sigmoid((f_logit + f_bias).astype(jnp.float32))
    oc = forgetting_attention(heads(qc, HEADS_C), heads(kc, HEADS_C), heads(vc, HEADS_C),
                              log_f).reshape(B, S, WIDTH) * jax.nn.silu(gc)

    y = jnp.einsum('nbsw,nwd->bsnd', jnp.stack([oa, ob, oc], axis=0), w_branch)
    gates = jax.nn.sigmoid(merge_logit).reshape(B, S, N_BRANCH, D_MODEL)
    merged = jnp.sum(gates * y, axis=2)
    return x + jnp.einsum('bsd,de->bse', merged, w_out)


def setup_inputs(seed: int = 0) -> dict:
    key = jax.random.key(seed)
    ks = jax.random.split(key, 8)
    x = jax.random.normal(ks[0], (BATCH, SEQ, D_MODEL), jnp.float32)
    norm_gain = 1.0 + 0.05 * jax.random.normal(ks[1], (DEPTH, D_MODEL), jnp.float32)
    w_in = jax.random.normal(ks[2], (DEPTH, D_MODEL, D_IN), jnp.float32) * D_MODEL ** -0.5
    forget_bias = 3.0 + 0.5 * jax.random.normal(ks[3], (DEPTH, HEADS_C), jnp.float32)
    w_branch = jax.random.normal(ks[4], (DEPTH, N_BRANCH, WIDTH, D_MODEL), jnp.float32) * WIDTH ** -0.5
    w_out = jax.random.normal(ks[5], (DEPTH, D_MODEL, D_MODEL), jnp.float32) * (0.5 * D_MODEL ** -0.5)
    final_gain = 1.0 + 0.05 * jax.random.normal(ks[6], (D_MODEL,), jnp.float32)
    return {"x": x, "norm_gain": norm_gain, "w_in": w_in, "forget_bias": forget_bias,
            "w_branch": w_branch, "w_out": w_out, "final_gain": final_gain}


def reference(x, norm_gain, w_in, forget_bias, w_branch, w_out, final_gain):
    h = x
    for layer in range(DEPTH):
        h = hybrid_layer(h, norm_gain[layer], w_in[layer], forget_bias[layer],
                         w_branch[layer], w_out[layer])
    return rms_norm(h, final_gain)
```

```python
import functools

import jax
import jax.numpy as jnp
from jax import lax
from jax.experimental import pallas as pl
from jax.experimental.pallas import tpu as pltpu

HEAD_DIM = 64
N_HEADS = 8
WIDTH = N_HEADS * HEAD_DIM
N_BRANCH = 3
ROPE_DIM = HEAD_DIM // 4
ROPE_THETA = 500000.0
MOBA_BLOCK = 256
MOBA_TOPK = 3
IDX_HEADS = 8
IDX_DIM = 64
DSA_TOPK_MAX = 256
RMS_EPS = 1e-6
ATTN_SCALE = HEAD_DIM ** -0.5
IDX_SCALE = IDX_HEADS ** -0.5 * IDX_DIM ** -0.5

LANE = 128
VMEM_LIMIT_BYTES = 48 * 1024 * 1024

TQ = 256
TK = 256
PROJ_TM = 1024
OUT_TM = 512
CSUM_CHUNK = 512

NEG = -1e30
INT_MIN = -(2 ** 31)

SEC_QA, SEC_KA, SEC_QB, SEC_KB, SEC_QIDX = 0, 1, 2, 3, 4
SEC_VA, SEC_VB, SEC_QC, SEC_KC, SEC_VC = 5, 6, 7, 8, 9
SEC_MERGE = 10
SEC_GA, SEC_GB, SEC_GC = 16, 17, 18
N_SEC = 19
SMALL_KIDX, SMALL_WIDX, SMALL_F = 0, 64, 72

f32 = jnp.float32
bf16 = jnp.bfloat16


def _nt_dot(a, b):
    return lax.dot_general(a, b, (((1,), (1,)), ((), ())), preferred_element_type=f32)


def _rope(x, cos, sin_a, sin_b):
    return x * cos + pltpu.roll(x, LANE - ROPE_DIM // 2, 1) * sin_a + pltpu.roll(x, ROPE_DIM // 2, 1) * sin_b


def _proj_kernel(x_ref, gain_ref, w_ref, ws_ref, cos_ref, sa_ref, sb_ref, fb_ref,
                 p_ref, small_ref, h_ref):
    j = pl.program_id(1)
    cos, sa, sb = cos_ref[...], sa_ref[...], sb_ref[...]

    @pl.when(j == 0)
    def _():
        xf = x_ref[...]
        ms = jnp.mean(xf * xf, axis=-1, keepdims=True)
        h_ref[...] = (xf * lax.rsqrt(ms + RMS_EPS) * gain_ref[...]).astype(bf16)
        sm = jnp.dot(h_ref[...], ws_ref[...], preferred_element_type=f32)
        lane = lax.broadcasted_iota(jnp.int32, sm.shape, 1)
        z = sm + fb_ref[...]
        log_f = jnp.minimum(z, 0.0) - jnp.log1p(jnp.exp(-jnp.abs(z)))
        small_ref[...] = jnp.where(lane < SMALL_WIDX, _rope(sm, cos, sa, sb),
                                   jnp.where(lane < SMALL_F, sm * IDX_SCALE, log_f))

    acc = jnp.dot(h_ref[...], w_ref[...], preferred_element_type=f32)

    @pl.when(j <= SEC_QIDX)
    def _():
        scale = jnp.where((j == SEC_QA) | (j == SEC_QB), ATTN_SCALE, 1.0).astype(f32)
        for c in range(WIDTH // LANE):
            sl = slice(c * LANE, (c + 1) * LANE)
            p_ref[:, sl] = (_rope(acc[:, sl], cos, sa, sb) * scale).astype(bf16)

    @pl.when((j > SEC_QIDX) & (j < SEC_MERGE))
    def _():
        scale = jnp.where(j == SEC_QC, ATTN_SCALE, 1.0).astype(f32)
        p_ref[...] = (acc * scale).astype(bf16)

    @pl.when((j >= SEC_MERGE) & (j < SEC_GA))
    def _():
        p_ref[...] = jax.nn.sigmoid(acc).astype(bf16)

    @pl.when(j >= SEC_GA)
    def _():
        p_ref[...] = (acc * jax.nn.sigmoid(acc)).astype(bf16)


def _project(x2d, gain, w_main, w_small, cos, sin_a, sin_b, fbias, seq):
    n_rows, d_model = x2d.shape
    tm = min(PROJ_TM, seq)
    n_seq_tiles = seq // tm
    return pl.pallas_call(
        _proj_kernel,
        out_shape=(jax.ShapeDtypeStruct((n_rows, N_SEC * WIDTH), bf16),
                   jax.ShapeDtypeStruct((n_rows, LANE), f32)),
        grid=(n_rows // tm, N_SEC),
        in_specs=[
            pl.BlockSpec((tm, d_model), lambda i, j: (i, 0)),
            pl.BlockSpec((1, d_model), lambda i, j: (0, 0)),
            pl.BlockSpec((d_model, WIDTH), lambda i, j: (0, j)),
            pl.BlockSpec((d_model, LANE), lambda i, j: (0, 0)),
            pl.BlockSpec((tm, LANE), lambda i, j: (i % n_seq_tiles, 0)),
            pl.BlockSpec((tm, LANE), lambda i, j: (i % n_seq_tiles, 0)),
            pl.BlockSpec((tm, LANE), lambda i, j: (i % n_seq_tiles, 0)),
            pl.BlockSpec((1, LANE), lambda i, j: (0, 0)),
        ],
        out_specs=(pl.BlockSpec((tm, WIDTH), lambda i, j: (i, j)),
                   pl.BlockSpec((tm, LANE), lambda i, j: (i, 0))),
        scratch_shapes=[pltpu.VMEM((tm, d_model), bf16)],
        compiler_params=pltpu.CompilerParams(
            dimension_semantics=("arbitrary", "arbitrary"), vmem_limit_bytes=VMEM_LIMIT_BYTES),
        name="proj",
    )(x2d, gain, w_main, w_small, cos, sin_a, sin_b, fbias)


def _cumsum_kernel(x_ref, o_ref, carry_ref):
    @pl.when(pl.program_id(1) == 0)
    def _():
        carry_ref[...] = jnp.zeros_like(carry_ref)

    x = x_ref[...]
    ch = x.shape[0]
    tri = (lax.broadcasted_iota(jnp.int32, (ch, ch), 1)
           <= lax.broadcasted_iota(jnp.int32, (ch, ch), 0)).astype(bf16)
    hi = x.astype(bf16)
    r1 = x - hi.astype(f32)
    mid = r1.astype(bf16)
    lo = (r1 - mid.astype(f32)).astype(bf16)
    cs = (jnp.dot(tri, hi, preferred_element_type=f32) + jnp.dot(tri, mid, preferred_element_type=f32)
          + jnp.dot(tri, lo, preferred_element_type=f32)) + carry_ref[0:1, :]
    o_ref[...] = cs
    carry_ref[...] = jnp.broadcast_to(cs[ch - 1:ch, :], carry_ref.shape)


def _cumsum(small3d):
    b, s, _ = small3d.shape
    ch = min(CSUM_CHUNK, s)
    return pl.pallas_call(
        _cumsum_kernel,
        out_shape=jax.ShapeDtypeStruct(small3d.shape, f32),
        grid=(b, s // ch),
        in_specs=[pl.BlockSpec((None, ch, LANE), lambda i, c: (i, c, 0))],
        out_specs=pl.BlockSpec((None, ch, LANE), lambda i, c: (i, c, 0)),
        scratch_shapes=[pltpu.VMEM((8, LANE), f32)],
        compiler_params=pltpu.CompilerParams(
            dimension_semantics=("arbitrary", "arbitrary"), vmem_limit_bytes=VMEM_LIMIT_BYTES),
        name="csum",
    )(small3d)


def _head_mask(q2, h):
    lane = lax.broadcasted_iota(jnp.int32, q2.shape, 1)
    return jnp.where((lane >= HEAD_DIM * h) & (lane < HEAD_DIM * (h + 1)), q2, jnp.zeros_like(q2))


def _online_update(s, vb, m, l, acc):
    m_new = jnp.maximum(m, jnp.max(s, axis=1, keepdims=True))
    p = jnp.exp(s - m_new)
    alpha = jnp.exp(m - m_new)
    l = alpha * l + jnp.sum(p, axis=1, keepdims=True)
    acc = alpha * acc + jnp.dot(p.astype(bf16), vb, preferred_element_type=f32)
    return m_new, l, acc


def _merge_heads(o0, o1):
    lane = lax.broadcasted_iota(jnp.int32, o0.shape, 1)
    return jnp.where(lane < HEAD_DIM, o0, o1)


def _fox_kernel(q_ref, k_ref, v_ref, ck_ref, o_ref):
    t = pl.program_id(2)
    q2 = q_ref[...]
    d0 = pl.multiple_of(t * TQ, TQ)
    causal = (lax.broadcasted_iota(jnp.int32, (TQ, TK), 1) <= lax.broadcasted_iota(jnp.int32, (TQ, TK), 0))
    outs = []
    for h in range(2):
        qm = _head_mask(q2, h)
        ck_d = ck_ref[h:h + 1, pl.ds(d0, TQ)]
        c0 = ck_d[:, 0:1]
        s = _nt_dot(qm, k_ref[pl.ds(d0, TK), :]) - (ck_d - c0)
        s = jnp.where(causal, s, NEG)
        m = jnp.max(s, axis=1, keepdims=True)
        p = jnp.exp(s - m)
        l = jnp.sum(p, axis=1, keepdims=True)
        acc = jnp.dot(p.astype(bf16), v_ref[pl.ds(d0, TK), :], preferred_element_type=f32)

        def body(j, carry, qm=qm, c0=c0, h=h):
            off = pl.multiple_of(j * TK, TK)
            s = _nt_dot(qm, k_ref[pl.ds(off, TK), :]) - (ck_ref[h:h + 1, pl.ds(off, TK)] - c0)
            return _online_update(s, v_ref[pl.ds(off, TK), :], *carry)

        m, l, acc = lax.fori_loop(0, t, body, (m, l, acc))
        outs.append(acc / l)
    o_ref[...] = _merge_heads(*outs).astype(bf16)


def _fox(p3d, ck):
    b, s, _ = p3d.shape
    cpb = WIDTH // LANE
    return pl.pallas_call(
        _fox_kernel,
        out_shape=jax.ShapeDtypeStruct((b, s, WIDTH), bf16),
        grid=(b, cpb, s // TQ),
        in_specs=[
            pl.BlockSpec((None, TQ, LANE), lambda i, p, t: (i, t, SEC_QC * cpb + p)),
            pl.BlockSpec((None, s, LANE), lambda i, p, t: (i, 0, SEC_KC * cpb + p)),
            pl.BlockSpec((None, s, LANE), lambda i, p, t: (i, 0, SEC_VC * cpb + p)),
            pl.BlockSpec((None, None, 2, s), lambda i, p, t: (i, p, 0, 0)),
        ],
        out_specs=pl.BlockSpec((None, TQ, LANE), lambda i, p, t: (i, t, p)),
        compiler_params=pltpu.CompilerParams(
            dimension_semantics=("arbitrary", "arbitrary", "arbitrary"), vmem_limit_bytes=VMEM_LIMIT_BYTES),
        name="fox",
    )(p3d, p3d, p3d, ck)


def _moba_kernel(q_ref, k_ref, v_ref, o_ref, km_ref, *, n_blk):
    t = pl.program_id(2)

    @pl.when(t == 0)
    def _():
        km_ref[...] = jnp.zeros_like(km_ref)
        for blk in range(n_blk):
            kb = k_ref[blk * MOBA_BLOCK:(blk + 1) * MOBA_BLOCK, :].astype(f32)
            km_ref[blk:blk + 1, :] = jnp.sum(kb, axis=0, keepdims=True) * (1.0 / MOBA_BLOCK)

    q2 = q_ref[...]
    kmean = km_ref[...].astype(bf16)
    d0 = pl.multiple_of(t * TQ, TQ)
    causal = (lax.broadcasted_iota(jnp.int32, (TQ, TK), 1) <= lax.broadcasted_iota(jnp.int32, (TQ, TK), 0))
    lane = lax.broadcasted_iota(jnp.int32, (TQ, LANE), 1)
    lane_f = lane.astype(f32)
    outs = []
    for h in range(2):
        qm = _head_mask(q2, h)
        g = jnp.where(lane < t, _nt_dot(qm, kmean), NEG)
        sel = jnp.zeros((TQ, LANE), f32)
        for _ in range(min(MOBA_TOPK, n_blk - 1)):
            mx = jnp.max(g, axis=1, keepdims=True)
            first = jnp.min(jnp.where(g == mx, lane_f, float(LANE)), axis=1, keepdims=True)
            hit = lane_f == first
            sel = jnp.where(hit, jnp.where(mx > 0.5 * NEG, 1.0, sel), sel)
            g = jnp.where(hit, NEG, g)

        s = jnp.where(causal, _nt_dot(qm, k_ref[pl.ds(d0, TK), :]), NEG)
        m = jnp.max(s, axis=1, keepdims=True)
        p = jnp.exp(s - m)
        l = jnp.sum(p, axis=1, keepdims=True)
        acc = jnp.dot(p.astype(bf16), v_ref[pl.ds(d0, TK), :], preferred_element_type=f32)

        def body(j, carry, qm=qm, sel=sel):
            m, l, acc = carry
            off = pl.multiple_of(j * TK, TK)
            picked = jnp.sum(jnp.where(lane == j, sel, 0.0), axis=1, keepdims=True) > 0.5
            s = _nt_dot(qm, k_ref[pl.ds(off, TK), :])
            m_new = jnp.maximum(m, jnp.where(picked, jnp.max(s, axis=1, keepdims=True), NEG))
            p = jnp.exp(s - jnp.where(picked, m_new, -NEG))
            alpha = jnp.exp(m - m_new)
            l = alpha * l + jnp.sum(p, axis=1, keepdims=True)
            acc = alpha * acc + jnp.dot(p.astype(bf16), v_ref[pl.ds(off, TK), :], preferred_element_type=f32)
            return m_new, l, acc

        m, l, acc = lax.fori_loop(0, t, body, (m, l, acc))
        outs.append(acc / l)
    o_ref[...] = _merge_heads(*outs).astype(bf16)


def _moba(p3d):
    b, s, _ = p3d.shape
    assert TQ == MOBA_BLOCK and TK == MOBA_BLOCK and s % MOBA_BLOCK == 0
    n_blk = s // MOBA_BLOCK
    assert n_blk <= LANE
    cpb = WIDTH // LANE
    return pl.pallas_call(
        functools.partial(_moba_kernel, n_blk=n_blk),
        out_shape=jax.ShapeDtypeStruct((b, s, WIDTH), bf16),
        grid=(b, cpb, n_blk),
        in_specs=[
            pl.BlockSpec((None, TQ, LANE), lambda i, p, t: (i, t, SEC_QA * cpb + p)),
            pl.BlockSpec((None, s, LANE), lambda i, p, t: (i, 0, SEC_KA * cpb + p)),
            pl.BlockSpec((None, s, LANE), lambda i, p, t: (i, 0, SEC_VA * cpb + p)),
        ],
        out_specs=pl.BlockSpec((None, TQ, LANE), lambda i, p, t: (i, t, p)),
        scratch_shapes=[pltpu.VMEM((LANE, LANE), f32)],
        compiler_params=pltpu.CompilerParams(
            dimension_semantics=("arbitrary", "arbitrary", "arbitrary"), vmem_limit_bytes=VMEM_LIMIT_BYTES),
        name="moba",
    )(p3d, p3d, p3d)


def _dsa_kernel(q_ref, k_ref, v_ref, qi_ref, kk_ref, sm_ref, o_ref, key_ref, *, top, seq):
    t = pl.program_id(1)
    n_kv = t + 1
    causal = (lax.broadcasted_iota(jnp.int32, (TQ, TK), 1) <= lax.broadcasted_iota(jnp.int32, (TQ, TK), 0))
    col = lax.broadcasted_iota(jnp.int32, (TQ, TK), 1)
    w_all = sm_ref[...]

    def score_body(j, _):
        off = pl.multiple_of(j * TK, TK)
        kkb = kk_ref[pl.ds(off, TK), :]
        isc = jnp.zeros((TQ, TK), f32)
        for hh in range(IDX_HEADS):
            slab = qi_ref[:, (hh // 2) * LANE:(hh // 2 + 1) * LANE]
            r = _nt_dot(_head_mask(slab, hh % 2), kkb)
            isc = isc + w_all[:, SMALL_WIDX + hh:SMALL_WIDX + hh + 1] * jnp.maximum(r, 0.0)
        bits = pltpu.bitcast(isc, jnp.int32)
        bits = jnp.where(bits == INT_MIN, 0, bits)
        key = jnp.where(bits < 0, bits ^ 0x7FFFFFFF, bits)
        key = jnp.where((j < t) | causal, key, INT_MIN)
        key_ref[:, pl.ds(off, TK)] = key
        return 0

    lax.fori_loop(0, n_kv, score_body, 0)

    def count(pred):
        def body(j, acc):
            kb = key_ref[:, pl.ds(pl.multiple_of(j * TK, TK), TK)]
            hit = jnp.where(pred(kb, j), 1.0, 0.0)
            for c in range(TK // LANE):
                acc = acc + hit[:, c * LANE:(c + 1) * LANE]
            return acc
        acc = lax.fori_loop(0, n_kv, body, jnp.zeros((TQ, LANE), f32))
        return jnp.sum(acc, axis=1, keepdims=True)

    k_top = float(top)
    c_pos = count(lambda kb, j: kb >= 0)
    thr = jnp.where(c_pos >= k_top, 0, INT_MIN).astype(jnp.int32)
    n_ge = jnp.where(c_pos >= k_top, c_pos, (n_kv * TK).astype(f32))

    def bit_body(i, carry):
        thr, n_ge = carry
        cand = thr + jnp.left_shift(jnp.int32(1), 30 - i)
        c = count(lambda kb, j: kb >= cand)
        ok = c >= k_top
        return jnp.where(ok, cand, thr), jnp.where(ok, c, n_ge)

    thr, n_ge = lax.fori_loop(0, 31, bit_body, (thr, n_ge))

    tie_cut_ref = jnp.where(thr == INT_MIN, 0, seq + 1).astype(jnp.int32)
    over = jnp.max(jnp.where((thr > INT_MIN) & (n_ge > k_top), 1.0, 0.0))

    def tie_cut():
        need = k_top - count(lambda kb, j: kb > thr)
        def cut_body(i, cut):
            cand = cut + jnp.left_shift(jnp.int32(1), (seq.bit_length() - 1) - i)
            c = count(lambda kb, j: (kb == thr) & (col + j * TK < cand))
            return jnp.where(c <= need, cand, cut)
        cut = lax.fori_loop(0, seq.bit_length(), cut_body, jnp.zeros((TQ, 1), jnp.int32))
        return jnp.where(thr == INT_MIN, 0, cut)

    cut = lax.cond(over > 0.5, tie_cut, lambda: tie_cut_ref)

    def bias_body(j, _):
        off = pl.multiple_of(j * TK, TK)
        kb = key_ref[:, pl.ds(off, TK)]
        keep = (kb > thr) | ((kb == thr) & (col + j * TK < cut))
        key_ref[:, pl.ds(off, TK)] = pltpu.bitcast(jnp.where(keep, 0.0, NEG).astype(f32), jnp.int32)
        return 0

    lax.fori_loop(0, n_kv, bias_body, 0)

    for slab in range(WIDTH // LANE):
        cols = slice(slab * LANE, (slab + 1) * LANE)
        q2 = q_ref[:, cols]
        outs = []
        for h in range(2):
            qm = _head_mask(q2, h)

            def body(j, carry, qm=qm, cols=cols):
                off = pl.multiple_of(j * TK, TK)
                s = _nt_dot(qm, k_ref[pl.ds(off, TK), cols]) + pltpu.bitcast(key_ref[:, pl.ds(off, TK)], f32)
                return _online_update(s, v_ref[pl.ds(off, TK), cols], *carry)

            init = (jnp.full((TQ, 1), NEG, f32), jnp.zeros((TQ, 1), f32), jnp.zeros((TQ, LANE), f32))
            m, l, acc = lax.fori_loop(0, n_kv, body, init)
            outs.append(acc / l)
        o_ref[:, cols] = _merge_heads(*outs).astype(bf16)


def _dsa(p3d, kk, small3d):
    b, s, _ = p3d.shape
    top = min(DSA_TOPK_MAX, s // 4)
    return pl.pallas_call(
        functools.partial(_dsa_kernel, top=top, seq=s),
        out_shape=jax.ShapeDtypeStruct((b, s, WIDTH), bf16),
        grid=(b, s // TQ),
        in_specs=[
            pl.BlockSpec((None, TQ, WIDTH), lambda i, t: (i, t, SEC_QB)),
            pl.BlockSpec((None, s, WIDTH), lambda i, t: (i, 0, SEC_KB)),
            pl.BlockSpec((None, s, WIDTH), lambda i, t: (i, 0, SEC_VB)),
            pl.BlockSpec((None, TQ, WIDTH), lambda i, t: (i, t, SEC_QIDX)),
            pl.BlockSpec((None, s, LANE), lambda i, t: (i, 0, 0)),
            pl.BlockSpec((None, TQ, LANE), lambda i, t: (i, t, 0)),
        ],
        out_specs=pl.BlockSpec((None, TQ, WIDTH), lambda i, t: (i, t, 0)),
        scratch_shapes=[pltpu.VMEM((TQ, s), jnp.int32)],
        compiler_params=pltpu.CompilerParams(
            dimension_semantics=("arbitrary", "arbitrary"), vmem_limit_bytes=VMEM_LIMIT_BYTES),
        name="dsa",
    )(p3d, p3d, p3d, p3d, kk, small3d)


def _out_kernel(oa_ref, ob_ref, oc_ref, ga_ref, gb_ref, gc_ref, m0_ref, m1_ref, m2_ref,
                x_ref, wb_ref, wo_ref, fg_ref, y_ref, *, final_norm):
    merged = None
    for n, (o_ref, g_ref, mg_ref) in enumerate(((oa_ref, ga_ref, m0_ref), (ob_ref, gb_ref, m1_ref),
                                                (oc_ref, gc_ref, m2_ref))):
        gated = (o_ref[...].astype(f32) * g_ref[...].astype(f32)).astype(bf16)
        y = jnp.dot(gated, wb_ref[n], preferred_element_type=f32) * mg_ref[...].astype(f32)
        merged = y if merged is None else merged + y
    out = x_ref[...] + jnp.dot(merged.astype(bf16), wo_ref[...], preferred_element_type=f32)
    if final_norm:
        ms = jnp.mean(out * out, axis=-1, keepdims=True)
        out = out * lax.rsqrt(ms + RMS_EPS) * fg_ref[...]
    y_ref[...] = out


def _output(oa, ob, oc, p2d, x2d, w_branch, w_out, final_gain, final_norm):
    n_rows, d_model = x2d.shape
    tm = min(OUT_TM, n_rows)
    mg0 = SEC_MERGE * WIDTH // d_model
    assert SEC_MERGE * WIDTH % d_model == 0
    row = lambda c: (lambda i: (i, c))
    return pl.pallas_call(
        functools.partial(_out_kernel, final_norm=final_norm),
        out_shape=jax.ShapeDtypeStruct((n_rows, d_model), f32),
        grid=(n_rows // tm,),
        in_specs=[
            pl.BlockSpec((tm, WIDTH), row(0)), pl.BlockSpec((tm, WIDTH), row(0)), pl.BlockSpec((tm, WIDTH), row(0)),
            pl.BlockSpec((tm, WIDTH), row(SEC_GA)), pl.BlockSpec((tm, WIDTH), row(SEC_GB)),
            pl.BlockSpec((tm, WIDTH), row(SEC_GC)),
            pl.BlockSpec((tm, d_model), row(mg0)), pl.BlockSpec((tm, d_model), row(mg0 + 1)),
            pl.BlockSpec((tm, d_model), row(mg0 + 2)),
            pl.BlockSpec((tm, d_model), row(0)),
            pl.BlockSpec((N_BRANCH, WIDTH, d_model), lambda i: (0, 0, 0)),
            pl.BlockSpec((d_model, d_model), lambda i: (0, 0)),
            pl.BlockSpec((1, d_model), lambda i: (0, 0)),
        ],
        out_specs=pl.BlockSpec((tm, d_model), row(0)),
        compiler_params=pltpu.CompilerParams(
            dimension_semantics=("arbitrary",), vmem_limit_bytes=VMEM_LIMIT_BYTES),
        name="out",
    )(oa, ob, oc, p2d, p2d, p2d, p2d, p2d, p2d, x2d, w_branch, w_out, final_gain)


def _pack_w_in(w):
    sec = lambda o: w[:, o * WIDTH:(o + 1) * WIDTH]
    q_idx0 = 12 * WIDTH
    small0 = q_idx0 + IDX_HEADS * IDX_DIM
    merge0 = small0 + IDX_DIM + IDX_HEADS + N_HEADS
    order = (0, 1, 4, 5, 12, 2, 6, 8, 9, 10)
    main = jnp.concatenate([sec(o) for o in order] + [w[:, merge0:]] + [sec(3), sec(7), sec(11)], axis=1)
    small = jnp.pad(w[:, small0:merge0], ((0, 0), (0, LANE - (merge0 - small0))))
    return main.astype(bf16), small.astype(bf16)


def _rope_tables(seq):
    half = ROPE_DIM // 2
    inv_freq = jnp.power(jnp.float32(ROPE_THETA), -jnp.arange(0, ROPE_DIM, 2, dtype=f32) / ROPE_DIM)
    ang = jnp.arange(seq).astype(f32)[:, None] * inv_freq[None, :]
    cos, sin = jnp.cos(ang), jnp.sin(ang)
    pad = jnp.zeros((seq, HEAD_DIM - ROPE_DIM), f32)
    zero = jnp.zeros((seq, half), f32)
    tile = lambda a: jnp.tile(a, (1, LANE // HEAD_DIM))
    cos_t = tile(jnp.concatenate([cos, cos, pad + 1.0], axis=1))
    sin_a = tile(jnp.concatenate([-sin, zero, pad], axis=1))
    sin_b = tile(jnp.concatenate([zero, sin, pad], axis=1))
    return cos_t, sin_a, sin_b


def kernel(x, norm_gain, w_in, forget_bias, w_branch, w_out, final_gain):
    b, s, d_model = x.shape
    depth = norm_gain.shape[0]
    assert s % TQ == 0 and s % min(PROJ_TM, s) == 0 and s % min(CSUM_CHUNK, s) == 0
    cos_t, sin_a, sin_b = _rope_tables(s)
    h2d = x.reshape(b * s, d_model)
    for layer in range(depth):
        w_main, w_small = _pack_w_in(w_in[layer])
        fbias = jnp.zeros((1, LANE), f32).at[0, SMALL_F:SMALL_F + N_HEADS].set(forget_bias[layer])
        p2d, small2d = _project(h2d, norm_gain[layer][None, :], w_main, w_small, cos_t, sin_a, sin_b, fbias, s)
        p3d = p2d.reshape(b, s, N_SEC * WIDTH)
        small3d = small2d.reshape(b, s, LANE)
        csum = _cumsum(small3d)[:, :, SMALL_F:SMALL_F + N_HEADS]
        ck = csum.transpose(0, 2, 1).reshape(b, N_HEADS // 2, 2, s)
        k_idx = small3d[:, :, SMALL_KIDX:SMALL_KIDX + IDX_DIM].astype(bf16)
        kk = jnp.concatenate([k_idx, k_idx], axis=-1)
        oa = _moba(p3d)
        ob = _dsa(p3d, kk, small3d)
        oc = _fox(p3d, ck)
        h2d = _output(oa.reshape(b * s, WIDTH), ob.reshape(b * s, WIDTH), oc.reshape(b * s, WIDTH),
                      p2d, h2d, w_branch[layer].astype(bf16), w_out[layer].astype(bf16),
                      final_gain[None, :], layer == depth - 1)
    return h2d.reshape(b, s, d_model)
```

```python
import functools

import jax
import jax.numpy as jnp
from jax import lax
from jax.experimental import pallas as pl
from jax.experimental.pallas import tpu as pltpu

HEAD_DIM = 64
N_HEADS = 8
WIDTH = N_HEADS * HEAD_DIM
N_BRANCH = 3
ROPE_DIM = HEAD_DIM // 4
ROPE_THETA = 500000.0
MOBA_BLOCK = 256
MOBA_TOPK = 3
IDX_HEADS = 8
IDX_DIM = 64
DSA_TOPK_MAX = 256
RMS_EPS = 1e-6
ATTN_SCALE = HEAD_DIM ** -0.5
IDX_SCALE = IDX_HEADS ** -0.5 * IDX_DIM ** -0.5

LANE = 128
SUBLANE = 8
BF16_ROWS = 16
VMEM_LIMIT_BYTES = 48 * 1024 * 1024

TQ = 256
TK = 256
PROJ_TM = 1024
OUT_TM = 512
CSUM_CHUNK = 512
ACC_ROWS = HEAD_DIM + BF16_ROWS

NEG = -1e30
INT_MIN = -(2 ** 31)

SEC_QA, SEC_KA, SEC_QB, SEC_KB, SEC_QIDX = 0, 1, 2, 3, 4
SEC_QC, SEC_KC = 5, 6
SEC_GA, SEC_GB, SEC_GC = 7, 8, 9
SEC_MERGE = 10
N_SEC = 16
SMALL_KIDX, SMALL_WIDX, SMALL_F = 0, 64, 72
N_PIECES = 3

f32 = jnp.float32
bf16 = jnp.bfloat16


def _nt_dot(a, b):
    return lax.dot_general(a, b, (((1,), (1,)), ((), ())), preferred_element_type=f32)


def _split3(x):
    hi = x.astype(bf16)
    r1 = x - hi.astype(f32)
    mid = r1.astype(bf16)
    lo = (r1 - mid.astype(f32)).astype(bf16)
    return hi, mid, lo


def _rope(x, cos, sin_a, sin_b):
    return x * cos + pltpu.roll(x, LANE - ROPE_DIM // 2, 1) * sin_a + pltpu.roll(x, ROPE_DIM // 2, 1) * sin_b


def _rms_norm_bf16(x_ref, gain_ref):
    xf = x_ref[...]
    ms = jnp.mean(xf * xf, axis=-1, keepdims=True)
    return (xf * lax.rsqrt(ms + RMS_EPS) * gain_ref[...]).astype(bf16)


def _proj_kernel(x_ref, gain_ref, w_ref, ws_ref, cos_ref, sa_ref, sb_ref, fb_ref,
                 p_ref, small_ref, h_ref):
    j = pl.program_id(1)
    cos, sa, sb = cos_ref[...], sa_ref[...], sb_ref[...]

    @pl.when(j == 0)
    def _():
        h_ref[...] = _rms_norm_bf16(x_ref, gain_ref)
        sm = jnp.dot(h_ref[...], ws_ref[...], preferred_element_type=f32)
        lane = lax.broadcasted_iota(jnp.int32, sm.shape, 1)
        z = sm + fb_ref[...]
        log_f = jnp.minimum(z, 0.0) - jnp.log1p(jnp.exp(-jnp.abs(z)))
        small_ref[...] = jnp.where(lane < SMALL_WIDX, _rope(sm, cos, sa, sb),
                                   jnp.where(lane < SMALL_F, sm * IDX_SCALE, log_f))

    acc = jnp.dot(h_ref[...], w_ref[...], preferred_element_type=f32)

    @pl.when(j <= SEC_QIDX)
    def _():
        scale = jnp.where((j == SEC_QA) | (j == SEC_QB), ATTN_SCALE, 1.0).astype(f32)
        for c in range(WIDTH // LANE):
            sl = slice(c * LANE, (c + 1) * LANE)
            p_ref[:, sl] = (_rope(acc[:, sl], cos, sa, sb) * scale).astype(bf16)

    @pl.when((j > SEC_QIDX) & (j < SEC_GA))
    def _():
        scale = jnp.where(j == SEC_QC, ATTN_SCALE, 1.0).astype(f32)
        p_ref[...] = (acc * scale).astype(bf16)

    @pl.when((j >= SEC_GA) & (j < SEC_MERGE))
    def _():
        p_ref[...] = (acc * jax.nn.sigmoid(acc)).astype(bf16)

    @pl.when(j >= SEC_MERGE)
    def _():
        p_ref[...] = jax.nn.sigmoid(acc).astype(bf16)


def _project(x2d, gain, w_main, w_small, cos, sin_a, sin_b, fbias, seq):
    n_rows, d_model = x2d.shape
    tm = min(PROJ_TM, seq)
    n_seq_tiles = seq // tm
    return pl.pallas_call(
        _proj_kernel,
        out_shape=(jax.ShapeDtypeStruct((n_rows, N_SEC * WIDTH), bf16),
                   jax.ShapeDtypeStruct((n_rows, LANE), f32)),
        grid=(n_rows // tm, N_SEC),
        in_specs=[
            pl.BlockSpec((tm, d_model), lambda i, j: (i, 0)),
            pl.BlockSpec((1, d_model), lambda i, j: (0, 0)),
            pl.BlockSpec((d_model, WIDTH), lambda i, j: (0, j)),
            pl.BlockSpec((d_model, LANE), lambda i, j: (0, 0)),
            pl.BlockSpec((tm, LANE), lambda i, j: (i % n_seq_tiles, 0)),
            pl.BlockSpec((tm, LANE), lambda i, j: (i % n_seq_tiles, 0)),
            pl.BlockSpec((tm, LANE), lambda i, j: (i % n_seq_tiles, 0)),
            pl.BlockSpec((1, LANE), lambda i, j: (0, 0)),
        ],
        out_specs=(pl.BlockSpec((tm, WIDTH), lambda i, j: (i, j)),
                   pl.BlockSpec((tm, LANE), lambda i, j: (i, 0))),
        scratch_shapes=[pltpu.VMEM((tm, d_model), bf16)],
        compiler_params=pltpu.CompilerParams(
            dimension_semantics=("arbitrary", "arbitrary"), vmem_limit_bytes=VMEM_LIMIT_BYTES),
        name="proj",
    )(x2d, gain, w_main, w_small, cos, sin_a, sin_b, fbias)


def _vproj_kernel(x_ref, gain_ref, wt_ref, vt_ref, h_ref):
    @pl.when(pl.program_id(1) == 0)
    def _():
        h_ref[...] = _rms_norm_bf16(x_ref, gain_ref)

    vt_ref[...] = _nt_dot(wt_ref[...], h_ref[...]).astype(bf16)


def _project_values(x2d, gain, w_vt, seq):
    n_rows, d_model = x2d.shape
    tm = min(PROJ_TM, seq)
    return pl.pallas_call(
        _vproj_kernel,
        out_shape=jax.ShapeDtypeStruct((N_BRANCH * WIDTH, n_rows), bf16),
        grid=(n_rows // tm, N_BRANCH),
        in_specs=[
            pl.BlockSpec((tm, d_model), lambda i, j: (i, 0)),
            pl.BlockSpec((1, d_model), lambda i, j: (0, 0)),
            pl.BlockSpec((WIDTH, d_model), lambda i, j: (j, 0)),
        ],
        out_specs=pl.BlockSpec((WIDTH, tm), lambda i, j: (j, i)),
        scratch_shapes=[pltpu.VMEM((tm, d_model), bf16)],
        compiler_params=pltpu.CompilerParams(
            dimension_semantics=("arbitrary", "arbitrary"), vmem_limit_bytes=VMEM_LIMIT_BYTES),
        name="vproj",
    )(x2d, gain, w_vt)


def _cumsum_kernel(x_ref, e_ref, carry_ref):
    @pl.when(pl.program_id(1) == 0)
    def _():
        carry_ref[...] = jnp.zeros_like(carry_ref)

    x = x_ref[...]
    ch = x.shape[0]
    tri = (lax.broadcasted_iota(jnp.int32, (ch, ch), 1)
           <= lax.broadcasted_iota(jnp.int32, (ch, ch), 0)).astype(bf16)
    cs = sum(jnp.dot(tri, piece, preferred_element_type=f32) for piece in _split3(x)) + carry_ref[0:1, :]
    carry_ref[...] = jnp.broadcast_to(cs[ch - 1:ch, :], carry_ref.shape)

    src = lax.broadcasted_iota(jnp.int32, (LANE, WIDTH), 0) - SMALL_F
    dst = lax.broadcasted_iota(jnp.int32, (LANE, WIDTH), 1)
    is_head = (src >= 0) & (src < N_HEADS)
    base = (src >> 1) * LANE + (src & 1) * N_PIECES
    out = None
    for n, piece in enumerate(_split3(-cs)):
        place = jnp.where(is_head & (dst == base + n), 1.0, 0.0).astype(bf16)
        term = jnp.dot(piece, place, preferred_element_type=f32)
        out = term if out is None else out + term
    e_ref[...] = out.astype(bf16)


def _cumsum_extras(small3d):
    b, s, _ = small3d.shape
    ch = min(CSUM_CHUNK, s)
    return pl.pallas_call(
        _cumsum_kernel,
        out_shape=jax.ShapeDtypeStruct((b, s, WIDTH), bf16),
        grid=(b, s // ch),
        in_specs=[pl.BlockSpec((None, ch, LANE), lambda i, c: (i, c, 0))],
        out_specs=pl.BlockSpec((None, ch, WIDTH), lambda i, c: (i, c, 0)),
        scratch_shapes=[pltpu.VMEM((SUBLANE, LANE), f32)],
        compiler_params=pltpu.CompilerParams(
            dimension_semantics=("arbitrary", "arbitrary"), vmem_limit_bytes=VMEM_LIMIT_BYTES),
        name="csum",
    )(small3d)


def _head_mask(q2, h):
    lane = lax.broadcasted_iota(jnp.int32, q2.shape, 1)
    return jnp.where((lane >= HEAD_DIM * h) & (lane < HEAD_DIM * (h + 1)), q2, jnp.zeros_like(q2))


def _slab(h):
    return slice((h // 2) * LANE, (h // 2 + 1) * LANE)


def _queries_t(q_ref):
    row = lax.broadcasted_iota(jnp.int32, (LANE, TQ), 0)
    out = []
    for h in range(N_HEADS):
        if h % 2 == 0:
            qt = q_ref[:, _slab(h)].astype(f32).T
        first = HEAD_DIM * (h % 2)
        out.append(jnp.where((row >= first) & (row < first + HEAD_DIM), qt, 0.0).astype(bf16))
    return out


def _values_with_ones(vt_ref, h, off):
    return jnp.concatenate([vt_ref[h * HEAD_DIM:(h + 1) * HEAD_DIM, pl.ds(off, TK)],
                            jnp.ones((BF16_ROWS, TK), bf16)], axis=0)


def _online_update(s, m, vt, acc_ref, h, picked=None):
    mx = jnp.max(s, axis=0, keepdims=True)
    if picked is not None:
        mx = jnp.where(picked, mx, NEG)
    m_new = jnp.maximum(m, mx)
    shift = m_new if picked is None else jnp.where(picked, m_new, -NEG)
    p = jnp.exp(s - shift).astype(bf16)
    acc_ref[h] = jnp.exp(m - m_new) * acc_ref[h] + jnp.dot(vt, p, preferred_element_type=f32)
    return m_new


def _write_output(acc_ref, o_ref):
    for pair in range(N_HEADS // 2):
        halves = []
        for h in (2 * pair, 2 * pair + 1):
            a = acc_ref[h]
            halves.append(a[:HEAD_DIM] / a[HEAD_DIM:HEAD_DIM + 1])
        o_ref[:, pair * LANE:(pair + 1) * LANE] = jnp.concatenate(halves, axis=0).T.astype(bf16)


def _diag_keep():
    return lax.broadcasted_iota(jnp.int32, (TK, TQ), 0) <= lax.broadcasted_iota(jnp.int32, (TK, TQ), 1)


def _init_rows():
    return tuple(jnp.full((1, TQ), NEG, f32) for _ in range(N_HEADS))


def _fox_kernel(q_ref, k_ref, e_ref, vt_ref, o_ref, acc_ref):
    t = pl.program_id(1)
    d0 = pl.multiple_of(t * TQ, TQ)
    acc_ref[...] = jnp.zeros_like(acc_ref)
    keep = _diag_keep()
    row = lax.broadcasted_iota(jnp.int32, (LANE, TQ), 0)
    q_aug = []
    for h, qt in enumerate(_queries_t(q_ref)):
        first = (h % 2) * N_PIECES
        ones = jnp.where((row >= first) & (row < first + N_PIECES), 1.0, 0.0).astype(bf16)
        q_aug.append(jnp.concatenate([qt, ones], axis=0))

    def tile(off, ms, diagonal):
        scores = []
        for h in range(N_HEADS):
            if h % 2 == 0:
                k_aug = jnp.concatenate([k_ref[pl.ds(off, TK), _slab(h)], e_ref[pl.ds(off, TK), _slab(h)]], axis=1)
            scores.append(jnp.dot(k_aug, q_aug[h], preferred_element_type=f32))
        out = []
        for h, s in enumerate(scores):
            if diagonal:
                s = jnp.where(keep, s, NEG)
            out.append(_online_update(s, ms[h], _values_with_ones(vt_ref, h, off), acc_ref, h))
        return tuple(out)

    ms = tile(d0, _init_rows(), True)
    lax.fori_loop(0, t, lambda j, ms: tile(pl.multiple_of(j * TK, TK), ms, False), ms)
    _write_output(acc_ref, o_ref)


def _fox(p3d, extras, vt):
    b, s, _ = p3d.shape
    return pl.pallas_call(
        _fox_kernel,
        out_shape=jax.ShapeDtypeStruct((b, s, WIDTH), bf16),
        grid=(b, s // TQ),
        in_specs=[
            pl.BlockSpec((None, TQ, WIDTH), lambda i, t: (i, t, SEC_QC)),
            pl.BlockSpec((None, s, WIDTH), lambda i, t: (i, 0, SEC_KC)),
            pl.BlockSpec((None, s, WIDTH), lambda i, t: (i, 0, 0)),
            pl.BlockSpec((WIDTH, s), lambda i, t: (2, i)),
        ],
        out_specs=pl.BlockSpec((None, TQ, WIDTH), lambda i, t: (i, t, 0)),
        scratch_shapes=[pltpu.VMEM((N_HEADS, ACC_ROWS, TQ), f32)],
        compiler_params=pltpu.CompilerParams(
            dimension_semantics=("arbitrary", "arbitrary"), vmem_limit_bytes=VMEM_LIMIT_BYTES),
        name="fox",
    )(p3d, p3d, extras, vt)


def _moba_kernel(q_ref, k_ref, vt_ref, o_ref, acc_ref, km_ref, sel_ref, *, n_blk):
    t = pl.program_id(1)
    nb_pad = km_ref.shape[0]

    @pl.when(t == 0)
    def _():
        km_ref[...] = jnp.zeros_like(km_ref)
        for blk in range(n_blk):
            kb = k_ref[blk * MOBA_BLOCK:(blk + 1) * MOBA_BLOCK, :].astype(f32)
            km_ref[blk:blk + 1, :] = jnp.sum(kb, axis=0, keepdims=True) * (1.0 / MOBA_BLOCK)

    d0 = pl.multiple_of(t * TQ, TQ)
    acc_ref[...] = jnp.zeros_like(acc_ref)
    keep = _diag_keep()
    blk_row = lax.broadcasted_iota(jnp.int32, (nb_pad, TQ), 0)
    blk_row_f = blk_row.astype(f32)
    qt = _queries_t(q_ref)

    for h in range(N_HEADS):
        gate = jnp.dot(km_ref[:, _slab(h)].astype(bf16), qt[h], preferred_element_type=f32)
        g = jnp.where(blk_row < t, gate, NEG)
        sel = jnp.zeros((nb_pad, TQ), f32)
        for _ in range(min(MOBA_TOPK, n_blk - 1)):
            mx = jnp.max(g, axis=0, keepdims=True)
            first = jnp.min(jnp.where(g == mx, blk_row_f, float(nb_pad)), axis=0, keepdims=True)
            hit = blk_row_f == first
            sel = jnp.where(hit, jnp.where(mx > 0.5 * NEG, 1.0, sel), sel)
            g = jnp.where(hit, NEG, g)
        sel_ref[h] = sel

    def tile(off, ms, j):
        scores = [jnp.dot(k_ref[pl.ds(off, TK), _slab(h)], qt[h], preferred_element_type=f32)
                  for h in range(N_HEADS)]
        out = []
        for h, s in enumerate(scores):
            if j is None:
                s, picked = jnp.where(keep, s, NEG), None
            else:
                picked = sel_ref[h, pl.ds(j, 1), :] > 0.5
            out.append(_online_update(s, ms[h], _values_with_ones(vt_ref, h, off), acc_ref, h, picked))
        return tuple(out)

    ms = tile(d0, _init_rows(), None)
    lax.fori_loop(0, t, lambda j, ms: tile(pl.multiple_of(j * TK, TK), ms, j), ms)
    _write_output(acc_ref, o_ref)


def _moba(p3d, vt):
    b, s, _ = p3d.shape
    assert TQ == MOBA_BLOCK and TK == MOBA_BLOCK and s % MOBA_BLOCK == 0
    n_blk = s // MOBA_BLOCK
    nb_pad = -(-n_blk // BF16_ROWS) * BF16_ROWS
    return pl.pallas_call(
        functools.partial(_moba_kernel, n_blk=n_blk),
        out_shape=jax.ShapeDtypeStruct((b, s, WIDTH), bf16),
        grid=(b, n_blk),
        in_specs=[
            pl.BlockSpec((None, TQ, WIDTH), lambda i, t: (i, t, SEC_QA)),
            pl.BlockSpec((None, s, WIDTH), lambda i, t: (i, 0, SEC_KA)),
            pl.BlockSpec((WIDTH, s), lambda i, t: (0, i)),
        ],
        out_specs=pl.BlockSpec((None, TQ, WIDTH), lambda i, t: (i, t, 0)),
        scratch_shapes=[pltpu.VMEM((N_HEADS, ACC_ROWS, TQ), f32),
                        pltpu.VMEM((nb_pad, WIDTH), f32),
                        pltpu.VMEM((N_HEADS, nb_pad, TQ), f32)],
        compiler_params=pltpu.CompilerParams(
            dimension_semantics=("arbitrary", "arbitrary"), vmem_limit_bytes=VMEM_LIMIT_BYTES),
        name="moba",
    )(p3d, p3d, vt)


def _dsa_kernel(q_ref, k_ref, vt_ref, qi_ref, kk_ref, sm_ref, o_ref, acc_ref, key_ref, *, top, seq):
    t = pl.program_id(1)
    n_kv = t + 1
    keep = _diag_keep()
    key_row = lax.broadcasted_iota(jnp.int32, (TK, TQ), 0)
    w_rows = sm_ref[...].T
    qit = _queries_t(qi_ref)

    def score_body(j, _):
        off = pl.multiple_of(j * TK, TK)
        kkb = kk_ref[pl.ds(off, TK), :]
        rel = [jnp.dot(kkb, qit[hh], preferred_element_type=f32) for hh in range(IDX_HEADS)]
        isc = jnp.zeros((TK, TQ), f32)
        for hh in range(IDX_HEADS):
            isc = isc + w_rows[SMALL_WIDX + hh:SMALL_WIDX + hh + 1, :] * jnp.maximum(rel[hh], 0.0)
        bits = pltpu.bitcast(isc, jnp.int32)
        bits = jnp.where(bits == INT_MIN, 0, bits)
        key = jnp.where(bits < 0, bits ^ 0x7FFFFFFF, bits)
        key_ref[pl.ds(off, TK), :] = jnp.where((j < t) | keep, key, INT_MIN)
        return 0

    lax.fori_loop(0, n_kv, score_body, 0)

    def count(pred):
        def body(j, acc):
            kb = key_ref[pl.ds(pl.multiple_of(j * TK, TK), TK), :]
            hit = jnp.where(pred(kb, j), 1.0, 0.0)
            return acc + jnp.sum(hit.reshape(TK // SUBLANE, SUBLANE, TQ), axis=0)
        acc = lax.fori_loop(0, n_kv, body, jnp.zeros((SUBLANE, TQ), f32))
        return jnp.sum(acc, axis=0, keepdims=True)

    k_top = float(top)
    c_pos = count(lambda kb, j: kb >= 0)
    thr = jnp.where(c_pos >= k_top, 0, INT_MIN).astype(jnp.int32)
    n_ge = jnp.where(c_pos >= k_top, c_pos, (n_kv * TK).astype(f32))

    def bit_body(i, carry):
        thr, n_ge = carry
        cand = thr + jnp.left_shift(jnp.int32(1), 30 - i)
        c = count(lambda kb, j: kb >= cand)
        ok = c >= k_top
        return jnp.where(ok, cand, thr), jnp.where(ok, c, n_ge)

    thr, n_ge = lax.fori_loop(0, 31, bit_body, (thr, n_ge))

    no_cut = jnp.where(thr == INT_MIN, 0, seq + 1).astype(jnp.int32)
    over = jnp.max(jnp.where((thr > INT_MIN) & (n_ge > k_top), 1.0, 0.0))

    def tie_cut():
        need = k_top - count(lambda kb, j: kb > thr)
        n_bits = seq.bit_length()

        def cut_body(i, cut):
            cand = cut + jnp.left_shift(jnp.int32(1), (n_bits - 1) - i)
            c = count(lambda kb, j: (kb == thr) & (key_row + j * TK < cand))
            return jnp.where(c <= need, cand, cut)

        cut = lax.fori_loop(0, n_bits, cut_body, jnp.zeros((1, TQ), jnp.int32))
        return jnp.where(thr == INT_MIN, 0, cut)

    cut = lax.cond(over > 0.5, tie_cut, lambda: no_cut)

    def bias_body(j, _):
        off = pl.multiple_of(j * TK, TK)
        kb = key_ref[pl.ds(off, TK), :]
        chosen = (kb > thr) | ((kb == thr) & (key_row + j * TK < cut))
        key_ref[pl.ds(off, TK), :] = pltpu.bitcast(jnp.where(chosen, -NEG, NEG).astype(f32), jnp.int32)
        return 0

    lax.fori_loop(0, n_kv, bias_body, 0)

    acc_ref[...] = jnp.zeros_like(acc_ref)
    qt = _queries_t(q_ref)

    def tile(j, ms):
        off = pl.multiple_of(j * TK, TK)
        scores = [jnp.dot(k_ref[pl.ds(off, TK), _slab(h)], qt[h], preferred_element_type=f32)
                  for h in range(N_HEADS)]
        ceiling = pltpu.bitcast(key_ref[pl.ds(off, TK), :], f32)
        out = []
        for h, s in enumerate(scores):
            s = jnp.minimum(s, ceiling)
            out.append(_online_update(s, ms[h], _values_with_ones(vt_ref, h, off), acc_ref, h))
        return tuple(out)

    lax.fori_loop(0, n_kv, tile, _init_rows())
    _write_output(acc_ref, o_ref)


def _dsa(p3d, vt, kk, small3d):
    b, s, _ = p3d.shape
    top = min(DSA_TOPK_MAX, s // 4)
    return pl.pallas_call(
        functools.partial(_dsa_kernel, top=top, seq=s),
        out_shape=jax.ShapeDtypeStruct((b, s, WIDTH), bf16),
        grid=(b, s // TQ),
        in_specs=[
            pl.BlockSpec((None, TQ, WIDTH), lambda i, t: (i, t, SEC_QB)),
            pl.BlockSpec((None, s, WIDTH), lambda i, t: (i, 0, SEC_KB)),
            pl.BlockSpec((WIDTH, s), lambda i, t: (1, i)),
            pl.BlockSpec((None, TQ, WIDTH), lambda i, t: (i, t, SEC_QIDX)),
            pl.BlockSpec((None, s, LANE), lambda i, t: (i, 0, 0)),
            pl.BlockSpec((None, TQ, LANE), lambda i, t: (i, t, 0)),
        ],
        out_specs=pl.BlockSpec((None, TQ, WIDTH), lambda i, t: (i, t, 0)),
        scratch_shapes=[pltpu.VMEM((N_HEADS, ACC_ROWS, TQ), f32), pltpu.VMEM((s, TQ), jnp.int32)],
        compiler_params=pltpu.CompilerParams(
            dimension_semantics=("arbitrary", "arbitrary"), vmem_limit_bytes=VMEM_LIMIT_BYTES),
        name="dsa",
    )(p3d, p3d, vt, p3d, kk, small3d)


def _out_kernel(oa_ref, ob_ref, oc_ref, ga_ref, gb_ref, gc_ref, m0_ref, m1_ref, m2_ref,
                x_ref, wb_ref, wo_ref, fg_ref, y_ref, *, final_norm):
    merged = None
    for n, (o_ref, g_ref, mg_ref) in enumerate(((oa_ref, ga_ref, m0_ref), (ob_ref, gb_ref, m1_ref),
                                                (oc_ref, gc_ref, m2_ref))):
        gated = (o_ref[...].astype(f32) * g_ref[...].astype(f32)).astype(bf16)
        y = jnp.dot(gated, wb_ref[n], preferred_element_type=f32) * mg_ref[...].astype(f32)
        merged = y if merged is None else merged + y
    out = x_ref[...] + jnp.dot(merged.astype(bf16), wo_ref[...], preferred_element_type=f32)
    if final_norm:
        ms = jnp.mean(out * out, axis=-1, keepdims=True)
        out = out * lax.rsqrt(ms + RMS_EPS) * fg_ref[...]
    y_ref[...] = out


def _output(oa, ob, oc, p2d, x2d, w_branch, w_out, final_gain, final_norm):
    n_rows, d_model = x2d.shape
    tm = min(OUT_TM, n_rows)
    assert SEC_MERGE * WIDTH % d_model == 0
    mg0 = SEC_MERGE * WIDTH // d_model
    row = lambda c: (lambda i: (i, c))
    return pl.pallas_call(
        functools.partial(_out_kernel, final_norm=final_norm),
        out_shape=jax.ShapeDtypeStruct((n_rows, d_model), f32),
        grid=(n_rows // tm,),
        in_specs=[
            pl.BlockSpec((tm, WIDTH), row(0)), pl.BlockSpec((tm, WIDTH), row(0)), pl.BlockSpec((tm, WIDTH), row(0)),
            pl.BlockSpec((tm, WIDTH), row(SEC_GA)), pl.BlockSpec((tm, WIDTH), row(SEC_GB)),
            pl.BlockSpec((tm, WIDTH), row(SEC_GC)),
            pl.BlockSpec((tm, d_model), row(mg0)), pl.BlockSpec((tm, d_model), row(mg0 + 1)),
            pl.BlockSpec((tm, d_model), row(mg0 + 2)),
            pl.BlockSpec((tm, d_model), row(0)),
            pl.BlockSpec((N_BRANCH, WIDTH, d_model), lambda i: (0, 0, 0)),
            pl.BlockSpec((d_model, d_model), lambda i: (0, 0)),
            pl.BlockSpec((1, d_model), lambda i: (0, 0)),
        ],
        out_specs=pl.BlockSpec((tm, d_model), row(0)),
        compiler_params=pltpu.CompilerParams(
            dimension_semantics=("arbitrary",), vmem_limit_bytes=VMEM_LIMIT_BYTES),
        name="out",
    )(oa, ob, oc, p2d, p2d, p2d, p2d, p2d, p2d, x2d, w_branch, w_out, final_gain)


def _pack_w_in(w):
    sec = lambda o: w[:, o * WIDTH:(o + 1) * WIDTH]
    small0 = 12 * WIDTH + IDX_HEADS * IDX_DIM
    merge0 = small0 + IDX_DIM + IDX_HEADS + N_HEADS
    order = (0, 1, 4, 5, 12, 8, 9, 3, 7, 11)
    main = jnp.concatenate([sec(o) for o in order] + [w[:, merge0:]], axis=1)
    small = jnp.pad(w[:, small0:merge0], ((0, 0), (0, LANE - (merge0 - small0))))
    values_t = jnp.concatenate([sec(2), sec(6), sec(10)], axis=1).T
    return main.astype(bf16), small.astype(bf16), values_t.astype(bf16)


def _rope_tables(seq):
    half = ROPE_DIM // 2
    inv_freq = jnp.power(jnp.float32(ROPE_THETA), -jnp.arange(0, ROPE_DIM, 2, dtype=f32) / ROPE_DIM)
    ang = jnp.arange(seq).astype(f32)[:, None] * inv_freq[None, :]
    cos, sin = jnp.cos(ang), jnp.sin(ang)
    pad = jnp.zeros((seq, HEAD_DIM - ROPE_DIM), f32)
    zero = jnp.zeros((seq, half), f32)
    tile = lambda a: jnp.tile(a, (1, LANE // HEAD_DIM))
    cos_t = tile(jnp.concatenate([cos, cos, pad + 1.0], axis=1))
    sin_a = tile(jnp.concatenate([-sin, zero, pad], axis=1))
    sin_b = tile(jnp.concatenate([zero, sin, pad], axis=1))
    return cos_t, sin_a, sin_b


def kernel(x, norm_gain, w_in, forget_bias, w_branch, w_out, final_gain):
    b, s, d_model = x.shape
    depth = norm_gain.shape[0]
    assert s % TQ == 0 and s % min(PROJ_TM, s) == 0 and s % min(CSUM_CHUNK, s) == 0
    cos_t, sin_a, sin_b = _rope_tables(s)
    h2d = x.reshape(b * s, d_model)
    for layer in range(depth):
        w_main, w_small, w_vt = _pack_w_in(w_in[layer])
        gain = norm_gain[layer][None, :]
        fbias = jnp.zeros((1, LANE), f32).at[0, SMALL_F:SMALL_F + N_HEADS].set(forget_bias[layer])
        p2d, small2d = _project(h2d, gain, w_main, w_small, cos_t, sin_a, sin_b, fbias, s)
        vt = _project_values(h2d, gain, w_vt, s)
        p3d = p2d.reshape(b, s, N_SEC * WIDTH)
        small3d = small2d.reshape(b, s, LANE)
        extras = _cumsum_extras(small3d)
        k_idx = small3d[:, :, SMALL_KIDX:SMALL_KIDX + IDX_DIM].astype(bf16)
        kk = jnp.concatenate([k_idx, k_idx], axis=-1)
        oa = _moba(p3d, vt)
        ob = _dsa(p3d, vt, kk, small3d)
        oc = _fox(p3d, extras, vt)
        h2d = _output(oa.reshape(b * s, WIDTH), ob.reshape(b * s, WIDTH), oc.reshape(b * s, WIDTH),
                      p2d, h2d, w_branch[layer].astype(bf16), w_out[layer].astype(bf16),
                      final_gain[None, :], layer == depth - 1)
    return h2d.reshape(b, s, d_model)
```

```python
import functools

import jax
import jax.numpy as jnp
from jax import lax
from jax.experimental import pallas as pl
from jax.experimental.pallas import tpu as pltpu

HEAD_DIM = 64
N_HEADS = 8
WIDTH = N_HEADS * HEAD_DIM
N_BRANCH = 3
ROPE_DIM = HEAD_DIM // 4
ROPE_THETA = 500000.0
MOBA_BLOCK = 256
MOBA_TOPK = 3
IDX_HEADS = 8
IDX_DIM = 64
DSA_TOPK_MAX = 256
RMS_EPS = 1e-6
LOG2E = 1.4426950408889634
ATTN_SCALE = HEAD_DIM ** -0.5 * LOG2E
IDX_SCALE = IDX_HEADS ** -0.5 * IDX_DIM ** -0.5

LANE = 128
SUBLANE = 8
BF16_ROWS = 16
VMEM_LIMIT_BYTES = 48 * 1024 * 1024

TQ = 256
TK = 256
PROJ_TM = 2048
OUT_TM = 512
CSUM_CHUNK = 512
ACC_ROWS = HEAD_DIM + BF16_ROWS

NEG = -1e30
INT_MIN = -(2 ** 31)
COUNT_CHAINS = 8

SEC_QA, SEC_KA, SEC_QB, SEC_KB, SEC_QIDX = 0, 1, 2, 3, 4
SEC_QC, SEC_KC = 5, 6
SEC_GA, SEC_GB, SEC_GC = 7, 8, 9
SEC_MERGE = 10
N_SEC = 16
SMALL_KIDX, SMALL_WIDX, SMALL_F = 0, 64, 72
N_PIECES = 3

f32 = jnp.float32
bf16 = jnp.bfloat16


def _nt_dot(a, b):
    return lax.dot_general(a, b, (((1,), (1,)), ((), ())), preferred_element_type=f32)


def _split3(x):
    hi = x.astype(bf16)
    r1 = x - hi.astype(f32)
    mid = r1.astype(bf16)
    lo = (r1 - mid.astype(f32)).astype(bf16)
    return hi, mid, lo


def _rope(x, cos, sin_a, sin_b):
    return x * cos + pltpu.roll(x, LANE - ROPE_DIM // 2, 1) * sin_a + pltpu.roll(x, ROPE_DIM // 2, 1) * sin_b


def _rms_norm_bf16(x_ref, gain_ref):
    xf = x_ref[...]
    ms = jnp.mean(xf * xf, axis=-1, keepdims=True)
    return (xf * lax.rsqrt(ms + RMS_EPS) * gain_ref[...]).astype(bf16)


def _proj_kernel(x_ref, gain_ref, w_ref, ws_ref, cos_ref, sa_ref, sb_ref, fb_ref,
                 p_ref, small_ref, h_ref):
    j = pl.program_id(1)
    cos, sa, sb = cos_ref[...], sa_ref[...], sb_ref[...]

    @pl.when(j == 0)
    def _():
        h_ref[...] = _rms_norm_bf16(x_ref, gain_ref)
        sm = jnp.dot(h_ref[...], ws_ref[...], preferred_element_type=f32)
        lane = lax.broadcasted_iota(jnp.int32, sm.shape, 1)
        z = sm + fb_ref[...]
        log_f = jnp.minimum(z, 0.0) - jnp.log1p(jnp.exp(-jnp.abs(z)))
        small_ref[...] = jnp.where(lane < SMALL_WIDX, _rope(sm, cos, sa, sb),
                                   jnp.where(lane < SMALL_F, sm * IDX_SCALE, log_f))

    acc = jnp.dot(h_ref[...], w_ref[...], preferred_element_type=f32)

    @pl.when(j <= SEC_QIDX)
    def _():
        scale = jnp.where((j == SEC_QA) | (j == SEC_QB), ATTN_SCALE, 1.0).astype(f32)
        for c in range(WIDTH // LANE):
            sl = slice(c * LANE, (c + 1) * LANE)
            p_ref[:, sl] = (_rope(acc[:, sl], cos, sa, sb) * scale).astype(bf16)

    @pl.when((j > SEC_QIDX) & (j < SEC_GA))
    def _():
        scale = jnp.where(j == SEC_QC, ATTN_SCALE, 1.0).astype(f32)
        p_ref[...] = (acc * scale).astype(bf16)

    @pl.when((j >= SEC_GA) & (j < SEC_MERGE))
    def _():
        p_ref[...] = (acc * jax.nn.sigmoid(acc)).astype(bf16)

    @pl.when(j >= SEC_MERGE)
    def _():
        p_ref[...] = jax.nn.sigmoid(acc).astype(bf16)


def _project(x2d, gain, w_main, w_small, cos, sin_a, sin_b, fbias, seq):
    n_rows, d_model = x2d.shape
    tm = min(PROJ_TM, seq)
    n_seq_tiles = seq // tm
    return pl.pallas_call(
        _proj_kernel,
        out_shape=(jax.ShapeDtypeStruct((n_rows, N_SEC * WIDTH), bf16),
                   jax.ShapeDtypeStruct((n_rows, LANE), f32)),
        grid=(n_rows // tm, N_SEC),
        in_specs=[
            pl.BlockSpec((tm, d_model), lambda i, j: (i, 0)),
            pl.BlockSpec((1, d_model), lambda i, j: (0, 0)),
            pl.BlockSpec((d_model, WIDTH), lambda i, j: (0, j)),
            pl.BlockSpec((d_model, LANE), lambda i, j: (0, 0)),
            pl.BlockSpec((tm, LANE), lambda i, j: (i % n_seq_tiles, 0)),
            pl.BlockSpec((tm, LANE), lambda i, j: (i % n_seq_tiles, 0)),
            pl.BlockSpec((tm, LANE), lambda i, j: (i % n_seq_tiles, 0)),
            pl.BlockSpec((1, LANE), lambda i, j: (0, 0)),
        ],
        out_specs=(pl.BlockSpec((tm, WIDTH), lambda i, j: (i, j)),
                   pl.BlockSpec((tm, LANE), lambda i, j: (i, 0))),
        scratch_shapes=[pltpu.VMEM((tm, d_model), bf16)],
        compiler_params=pltpu.CompilerParams(
            dimension_semantics=("arbitrary", "arbitrary"), vmem_limit_bytes=VMEM_LIMIT_BYTES),
        name="proj",
    )(x2d, gain, w_main, w_small, cos, sin_a, sin_b, fbias)


def _vproj_kernel(x_ref, gain_ref, wt_ref, vt_ref, h_ref):
    @pl.when(pl.program_id(1) == 0)
    def _():
        h_ref[...] = _rms_norm_bf16(x_ref, gain_ref)

    vt_ref[...] = _nt_dot(wt_ref[...], h_ref[...]).astype(bf16)


def _project_values(x2d, gain, w_vt, seq):
    n_rows, d_model = x2d.shape
    tm = min(PROJ_TM, seq)
    return pl.pallas_call(
        _vproj_kernel,
        out_shape=jax.ShapeDtypeStruct((N_BRANCH * WIDTH, n_rows), bf16),
        grid=(n_rows // tm, N_BRANCH),
        in_specs=[
            pl.BlockSpec((tm, d_model), lambda i, j: (i, 0)),
            pl.BlockSpec((1, d_model), lambda i, j: (0, 0)),
            pl.BlockSpec((WIDTH, d_model), lambda i, j: (j, 0)),
        ],
        out_specs=pl.BlockSpec((WIDTH, tm), lambda i, j: (j, i)),
        scratch_shapes=[pltpu.VMEM((tm, d_model), bf16)],
        compiler_params=pltpu.CompilerParams(
            dimension_semantics=("arbitrary", "arbitrary"), vmem_limit_bytes=VMEM_LIMIT_BYTES),
        name="vproj",
    )(x2d, gain, w_vt)


def _cumsum_kernel(x_ref, e_ref, carry_ref):
    @pl.when(pl.program_id(1) == 0)
    def _():
        carry_ref[...] = jnp.zeros_like(carry_ref)

    x = x_ref[...]
    ch = x.shape[0]
    tri = (lax.broadcasted_iota(jnp.int32, (ch, ch), 1)
           <= lax.broadcasted_iota(jnp.int32, (ch, ch), 0)).astype(bf16)
    cs = sum(jnp.dot(tri, piece, preferred_element_type=f32) for piece in _split3(x)) + carry_ref[0:1, :]
    carry_ref[...] = jnp.broadcast_to(cs[ch - 1:ch, :], carry_ref.shape)

    src = lax.broadcasted_iota(jnp.int32, (LANE, WIDTH), 0) - SMALL_F
    dst = lax.broadcasted_iota(jnp.int32, (LANE, WIDTH), 1)
    is_head = (src >= 0) & (src < N_HEADS)
    base = (src >> 1) * LANE + (src & 1) * N_PIECES
    out = None
    for n, piece in enumerate(_split3(cs * -LOG2E)):
        place = jnp.where(is_head & (dst == base + n), 1.0, 0.0).astype(bf16)
        term = jnp.dot(piece, place, preferred_element_type=f32)
        out = term if out is None else out + term
    e_ref[...] = out.astype(bf16)


def _cumsum_extras(small3d):
    b, s, _ = small3d.shape
    ch = min(CSUM_CHUNK, s)
    return pl.pallas_call(
        _cumsum_kernel,
        out_shape=jax.ShapeDtypeStruct((b, s, WIDTH), bf16),
        grid=(b, s // ch),
        in_specs=[pl.BlockSpec((None, ch, LANE), lambda i, c: (i, c, 0))],
        out_specs=pl.BlockSpec((None, ch, WIDTH), lambda i, c: (i, c, 0)),
        scratch_shapes=[pltpu.VMEM((SUBLANE, LANE), f32)],
        compiler_params=pltpu.CompilerParams(
            dimension_semantics=("arbitrary", "arbitrary"), vmem_limit_bytes=VMEM_LIMIT_BYTES),
        name="csum",
    )(small3d)


def _head_mask(q2, h):
    lane = lax.broadcasted_iota(jnp.int32, q2.shape, 1)
    return jnp.where((lane >= HEAD_DIM * h) & (lane < HEAD_DIM * (h + 1)), q2, jnp.zeros_like(q2))


def _slab(h):
    return slice((h // 2) * LANE, (h // 2 + 1) * LANE)


def _queries_t(q_ref):
    row = lax.broadcasted_iota(jnp.int32, (LANE, TQ), 0)
    out = []
    for h in range(N_HEADS):
        if h % 2 == 0:
            qt = q_ref[:, _slab(h)].astype(f32).T
        first = HEAD_DIM * (h % 2)
        out.append(jnp.where((row >= first) & (row < first + HEAD_DIM), qt, 0.0).astype(bf16))
    return out


def _values_with_ones(vt_ref, h, off):
    return jnp.concatenate([vt_ref[h * HEAD_DIM:(h + 1) * HEAD_DIM, pl.ds(off, TK)],
                            jnp.ones((BF16_ROWS, TK), bf16)], axis=0)


def _online_update(s, m, vt, acc_ref, h, picked=None):
    mx = jnp.max(s, axis=0, keepdims=True)
    if picked is not None:
        mx = jnp.where(picked, mx, NEG)
    m_new = jnp.maximum(m, mx)
    shift = m_new if picked is None else jnp.where(picked, m_new, -NEG)
    p = jnp.exp2(s - shift).astype(bf16)
    acc_ref[h] = jnp.exp2(m - m_new) * acc_ref[h] + jnp.dot(vt, p, preferred_element_type=f32)
    return m_new


def _write_output(acc_ref, o_ref):
    for pair in range(N_HEADS // 2):
        halves = []
        for h in (2 * pair, 2 * pair + 1):
            a = acc_ref[h]
            halves.append(a[:HEAD_DIM] / a[HEAD_DIM:HEAD_DIM + 1])
        o_ref[:, pair * LANE:(pair + 1) * LANE] = jnp.concatenate(halves, axis=0).T.astype(bf16)


def _diag_keep():
    return lax.broadcasted_iota(jnp.int32, (TK, TQ), 0) <= lax.broadcasted_iota(jnp.int32, (TK, TQ), 1)


def _init_rows():
    return tuple(jnp.full((1, TQ), NEG, f32) for _ in range(N_HEADS))


def _for_key_tiles(n, scores, update, carry):
    def pair(i, c):
        j = 2 * i
        s_a, s_b = scores(j), scores(j + 1)
        return update(j + 1, s_b, update(j, s_a, c))

    carry = lax.fori_loop(0, lax.shift_right_logical(n, 1), pair, carry)
    return lax.cond((n & 1) == 1, lambda c: update(n - 1, scores(n - 1), c), lambda c: c, carry)


def _fox_kernel(q_ref, k_ref, e_ref, vt_ref, o_ref, acc_ref):
    t = pl.program_id(1)
    d0 = pl.multiple_of(t * TQ, TQ)
    acc_ref[...] = jnp.zeros_like(acc_ref)
    keep = _diag_keep()
    row = lax.broadcasted_iota(jnp.int32, (LANE, TQ), 0)
    q_aug = []
    for h, qt in enumerate(_queries_t(q_ref)):
        first = (h % 2) * N_PIECES
        ones = jnp.where((row >= first) & (row < first + N_PIECES), 1.0, 0.0).astype(bf16)
        q_aug.append(jnp.concatenate([qt, ones], axis=0))

    def scores(j):
        off = pl.multiple_of(j * TK, TK)
        out = []
        for h in range(N_HEADS):
            if h % 2 == 0:
                k_aug = jnp.concatenate([k_ref[pl.ds(off, TK), _slab(h)], e_ref[pl.ds(off, TK), _slab(h)]], axis=1)
            out.append(jnp.dot(k_aug, q_aug[h], preferred_element_type=f32))
        return out

    def update(j, sc, ms, diagonal=False):
        off = pl.multiple_of(j * TK, TK)
        out = []
        for h, s in enumerate(sc):
            if diagonal:
                s = jnp.where(keep, s, NEG)
            out.append(_online_update(s, ms[h], _values_with_ones(vt_ref, h, off), acc_ref, h))
        return tuple(out)

    ms = update(t, scores(t), _init_rows(), diagonal=True)
    _for_key_tiles(t, scores, update, ms)
    _write_output(acc_ref, o_ref)


def _fox(p3d, extras, vt):
    b, s, _ = p3d.shape
    return pl.pallas_call(
        _fox_kernel,
        out_shape=jax.ShapeDtypeStruct((b, s, WIDTH), bf16),
        grid=(b, s // TQ),
        in_specs=[
            pl.BlockSpec((None, TQ, WIDTH), lambda i, t: (i, t, SEC_QC)),
            pl.BlockSpec((None, s, WIDTH), lambda i, t: (i, 0, SEC_KC)),
            pl.BlockSpec((None, s, WIDTH), lambda i, t: (i, 0, 0)),
            pl.BlockSpec((WIDTH, s), lambda i, t: (2, i)),
        ],
        out_specs=pl.BlockSpec((None, TQ, WIDTH), lambda i, t: (i, t, 0)),
        scratch_shapes=[pltpu.VMEM((N_HEADS, ACC_ROWS, TQ), f32)],
        compiler_params=pltpu.CompilerParams(
            dimension_semantics=("arbitrary", "arbitrary"), vmem_limit_bytes=VMEM_LIMIT_BYTES),
        name="fox",
    )(p3d, p3d, extras, vt)


def _moba_kernel(q_ref, k_ref, vt_ref, o_ref, acc_ref, km_ref, sel_ref, *, n_blk):
    t = pl.program_id(1)
    nb_pad = km_ref.shape[0]

    @pl.when(t == 0)
    def _():
        km_ref[...] = jnp.zeros_like(km_ref)
        for blk in range(n_blk):
            kb = k_ref[blk * MOBA_BLOCK:(blk + 1) * MOBA_BLOCK, :].astype(f32)
            km_ref[blk:blk + 1, :] = jnp.sum(kb, axis=0, keepdims=True) * (1.0 / MOBA_BLOCK)

    d0 = pl.multiple_of(t * TQ, TQ)
    acc_ref[...] = jnp.zeros_like(acc_ref)
    keep = _diag_keep()
    blk_row = lax.broadcasted_iota(jnp.int32, (nb_pad, TQ), 0)
    blk_row_f = blk_row.astype(f32)
    qt = _queries_t(q_ref)

    for h in range(N_HEADS):
        gate = jnp.dot(km_ref[:, _slab(h)].astype(bf16), qt[h], preferred_element_type=f32)
        g = jnp.where(blk_row < t, gate, NEG)
        sel = jnp.zeros((nb_pad, TQ), f32)
        for _ in range(min(MOBA_TOPK, n_blk - 1)):
            mx = jnp.max(g, axis=0, keepdims=True)
            first = jnp.min(jnp.where(g == mx, blk_row_f, float(nb_pad)), axis=0, keepdims=True)
            hit = blk_row_f == first
            sel = jnp.where(hit, jnp.where(mx > 0.5 * NEG, 1.0, sel), sel)
            g = jnp.where(hit, NEG, g)
        sel_ref[h] = sel

    def scores(j):
        off = pl.multiple_of(j * TK, TK)
        return [jnp.dot(k_ref[pl.ds(off, TK), _slab(h)], qt[h], preferred_element_type=f32)
                for h in range(N_HEADS)]

    def update(j, sc, ms, diagonal=False):
        off = pl.multiple_of(j * TK, TK)
        out = []
        for h, s in enumerate(sc):
            if diagonal:
                s, picked = jnp.where(keep, s, NEG), None
            else:
                picked = sel_ref[h, pl.ds(j, 1), :] > 0.5
            out.append(_online_update(s, ms[h], _values_with_ones(vt_ref, h, off), acc_ref, h, picked))
        return tuple(out)

    ms = update(t, scores(t), _init_rows(), diagonal=True)
    _for_key_tiles(t, scores, update, ms)
    _write_output(acc_ref, o_ref)


def _moba(p3d, vt):
    b, s, _ = p3d.shape
    assert TQ == MOBA_BLOCK and TK == MOBA_BLOCK and s % MOBA_BLOCK == 0
    n_blk = s // MOBA_BLOCK
    nb_pad = -(-n_blk // BF16_ROWS) * BF16_ROWS
    return pl.pallas_call(
        functools.partial(_moba_kernel, n_blk=n_blk),
        out_shape=jax.ShapeDtypeStruct((b, s, WIDTH), bf16),
        grid=(b, n_blk),
        in_specs=[
            pl.BlockSpec((None, TQ, WIDTH), lambda i, t: (i, t, SEC_QA)),
            pl.BlockSpec((None, s, WIDTH), lambda i, t: (i, 0, SEC_KA)),
            pl.BlockSpec((WIDTH, s), lambda i, t: (0, i)),
        ],
        out_specs=pl.BlockSpec((None, TQ, WIDTH), lambda i, t: (i, t, 0)),
        scratch_shapes=[pltpu.VMEM((N_HEADS, ACC_ROWS, TQ), f32),
                        pltpu.VMEM((nb_pad, WIDTH), f32),
                        pltpu.VMEM((N_HEADS, nb_pad, TQ), f32)],
        compiler_params=pltpu.CompilerParams(
            dimension_semantics=("arbitrary", "arbitrary"), vmem_limit_bytes=VMEM_LIMIT_BYTES),
        name="moba",
    )(p3d, p3d, vt)


def _dsa_kernel(q_ref, k_ref, vt_ref, qi_ref, kk_ref, sm_ref, o_ref, acc_ref, sc_ref, *, top, seq):
    t = pl.program_id(1)
    n_kv = t + 1
    n_chunks = lax.shift_right_logical(n_kv + 1, 1)
    k_top = float(top)
    keep = _diag_keep()
    chunk_row = lax.broadcasted_iota(jnp.int32, (2 * TK, TQ), 0)
    w_rows = sm_ref[...].T
    qit = _queries_t(qi_ref)

    @pl.when((n_kv & 1) == 1)
    def _():
        sc_ref[pl.ds(pl.multiple_of(n_kv * TK, TK), TK), :] = jnp.full((TK, TQ), -jnp.inf, f32)

    def score_body(j, _):
        off = pl.multiple_of(j * TK, TK)
        kkb = kk_ref[pl.ds(off, TK), :]
        rel = [jnp.dot(kkb, qit[hh], preferred_element_type=f32) for hh in range(IDX_HEADS)]
        isc = jnp.zeros((TK, TQ), f32)
        for hh in range(IDX_HEADS):
            isc = isc + w_rows[SMALL_WIDX + hh:SMALL_WIDX + hh + 1, :] * jnp.maximum(rel[hh], 0.0)
        sc_ref[pl.ds(off, TK), :] = jnp.where((j < t) | keep, isc, -jnp.inf)
        return 0

    lax.fori_loop(0, n_kv, score_body, 0)

    def count(pred):
        def body(c, acc):
            off = pl.multiple_of(c * (2 * TK), 2 * TK)
            hit = jnp.where(pred(sc_ref[pl.ds(off, 2 * TK), :], off), 1.0, 0.0)
            return acc + jnp.sum(hit.reshape(COUNT_CHAINS, -1, SUBLANE, TQ), axis=1)
        acc = lax.fori_loop(0, n_chunks, body, jnp.zeros((COUNT_CHAINS, SUBLANE, TQ), f32))
        return jnp.sum(jnp.sum(acc, axis=0), axis=0, keepdims=True)

    def as_score(code):
        return pltpu.bitcast(jnp.where(code < 0, code ^ 0x7FFFFFFF, code), f32)

    c_pos = count(lambda sc, off: sc >= 0.0)
    code = jnp.where(c_pos >= k_top, 0, INT_MIN).astype(jnp.int32)
    n_ge = jnp.where(c_pos >= k_top, c_pos, (n_kv * TK).astype(f32))

    def bit_body(i, carry):
        code, n_ge = carry
        cand = code + jnp.left_shift(jnp.int32(1), 30 - i)
        cand_f = as_score(cand)
        c = count(lambda sc, off: sc >= cand_f)
        ok = (c >= k_top) & (cand_f > -jnp.inf)
        return jnp.where(ok, cand, code), jnp.where(ok, c, n_ge)

    code, n_ge = lax.fori_loop(0, 31, bit_body, (code, n_ge))
    has_thr = code > INT_MIN
    thr = jnp.where(has_thr, as_score(code), -jnp.inf)

    no_cut = jnp.where(has_thr, seq + 1, 0).astype(jnp.int32)
    over = jnp.max(jnp.where(has_thr & (n_ge > k_top), 1.0, 0.0))

    def tie_cut():
        need = k_top - count(lambda sc, off: sc > thr)
        n_bits = seq.bit_length()

        def cut_body(i, cut):
            cand = cut + jnp.left_shift(jnp.int32(1), (n_bits - 1) - i)
            c = count(lambda sc, off: (sc == thr) & (chunk_row + off < cand))
            return jnp.where(c <= need, cand, cut)

        cut = lax.fori_loop(0, n_bits, cut_body, jnp.zeros((1, TQ), jnp.int32))
        return jnp.where(has_thr, cut, 0)

    cut = lax.cond(over > 0.5, tie_cut, lambda: no_cut)

    def ceiling_body(c, _):
        off = pl.multiple_of(c * (2 * TK), 2 * TK)
        sc = sc_ref[pl.ds(off, 2 * TK), :]
        chosen = (sc > thr) | ((sc == thr) & (chunk_row + off < cut))
        sc_ref[pl.ds(off, 2 * TK), :] = jnp.where(chosen, -NEG, NEG).astype(f32)
        return 0

    lax.fori_loop(0, n_chunks, ceiling_body, 0)

    acc_ref[...] = jnp.zeros_like(acc_ref)
    qt = _queries_t(q_ref)

    def scores(j):
        off = pl.multiple_of(j * TK, TK)
        return [jnp.dot(k_ref[pl.ds(off, TK), _slab(h)], qt[h], preferred_element_type=f32)
                for h in range(N_HEADS)]

    def update(j, sc, ms):
        off = pl.multiple_of(j * TK, TK)
        ceiling = sc_ref[pl.ds(off, TK), :]
        out = []
        for h, s in enumerate(sc):
            s = jnp.minimum(s, ceiling)
            out.append(_online_update(s, ms[h], _values_with_ones(vt_ref, h, off), acc_ref, h))
        return tuple(out)

    _for_key_tiles(n_kv, scores, update, _init_rows())
    _write_output(acc_ref, o_ref)


def _dsa(p3d, vt, kk, small3d):
    b, s, _ = p3d.shape
    top = min(DSA_TOPK_MAX, s // 4)
    return pl.pallas_call(
        functools.partial(_dsa_kernel, top=top, seq=s),
        out_shape=jax.ShapeDtypeStruct((b, s, WIDTH), bf16),
        grid=(b, s // TQ),
        in_specs=[
            pl.BlockSpec((None, TQ, WIDTH), lambda i, t: (i, t, SEC_QB)),
            pl.BlockSpec((None, s, WIDTH), lambda i, t: (i, 0, SEC_KB)),
            pl.BlockSpec((WIDTH, s), lambda i, t: (1, i)),
            pl.BlockSpec((None, TQ, WIDTH), lambda i, t: (i, t, SEC_QIDX)),
            pl.BlockSpec((None, s, LANE), lambda i, t: (i, 0, 0)),
            pl.BlockSpec((None, TQ, LANE), lambda i, t: (i, t, 0)),
        ],
        out_specs=pl.BlockSpec((None, TQ, WIDTH), lambda i, t: (i, t, 0)),
        scratch_shapes=[pltpu.VMEM((N_HEADS, ACC_ROWS, TQ), f32), pltpu.VMEM((s, TQ), f32)],
        compiler_params=pltpu.CompilerParams(
            dimension_semantics=("arbitrary", "arbitrary"), vmem_limit_bytes=VMEM_LIMIT_BYTES),
        name="dsa",
    )(p3d, p3d, vt, p3d, kk, small3d)


def _out_kernel(oa_ref, ob_ref, oc_ref, ga_ref, gb_ref, gc_ref, m0_ref, m1_ref, m2_ref,
                x_ref, wb_ref, wo_ref, fg_ref, y_ref, *, final_norm):
    merged = None
    for n, (o_ref, g_ref, mg_ref) in enumerate(((oa_ref, ga_ref, m0_ref), (ob_ref, gb_ref, m1_ref),
                                                (oc_ref, gc_ref, m2_ref))):
        gated = (o_ref[...].astype(f32) * g_ref[...].astype(f32)).astype(bf16)
        y = jnp.dot(gated, wb_ref[n], preferred_element_type=f32) * mg_ref[...].astype(f32)
        merged = y if merged is None else merged + y
    out = x_ref[...] + jnp.dot(merged.astype(bf16), wo_ref[...], preferred_element_type=f32)
    if final_norm:
        ms = jnp.mean(out * out, axis=-1, keepdims=True)
        out = out * lax.rsqrt(ms + RMS_EPS) * fg_ref[...]
    y_ref[...] = out


def _output(oa, ob, oc, p2d, x2d, w_branch, w_out, final_gain, final_norm):
    n_rows, d_model = x2d.shape
    tm = min(OUT_TM, n_rows)
    assert SEC_MERGE * WIDTH % d_model == 0
    mg0 = SEC_MERGE * WIDTH // d_model
    row = lambda c: (lambda i: (i, c))
    return pl.pallas_call(
        functools.partial(_out_kernel, final_norm=final_norm),
        out_shape=jax.ShapeDtypeStruct((n_rows, d_model), f32),
        grid=(n_rows // tm,),
        in_specs=[
            pl.BlockSpec((tm, WIDTH), row(0)), pl.BlockSpec((tm, WIDTH), row(0)), pl.BlockSpec((tm, WIDTH), row(0)),
            pl.BlockSpec((tm, WIDTH), row(SEC_GA)), pl.BlockSpec((tm, WIDTH), row(SEC_GB)),
            pl.BlockSpec((tm, WIDTH), row(SEC_GC)),
            pl.BlockSpec((tm, d_model), row(mg0)), pl.BlockSpec((tm, d_model), row(mg0 + 1)),
            pl.BlockSpec((tm, d_model), row(mg0 + 2)),
            pl.BlockSpec((tm, d_model), row(0)),
            pl.BlockSpec((N_BRANCH, WIDTH, d_model), lambda i: (0, 0, 0)),
            pl.BlockSpec((d_model, d_model), lambda i: (0, 0)),
            pl.BlockSpec((1, d_model), lambda i: (0, 0)),
        ],
        out_specs=pl.BlockSpec((tm, d_model), row(0)),
        compiler_params=pltpu.CompilerParams(
            dimension_semantics=("arbitrary",), vmem_limit_bytes=VMEM_LIMIT_BYTES),
        name="out",
    )(oa, ob, oc, p2d, p2d, p2d, p2d, p2d, p2d, x2d, w_branch, w_out, final_gain)


def _pack_w_in(w):
    sec = lambda o: w[:, o * WIDTH:(o + 1) * WIDTH]
    small0 = 12 * WIDTH + IDX_HEADS * IDX_DIM
    merge0 = small0 + IDX_DIM + IDX_HEADS + N_HEADS
    order = (0, 1, 4, 5, 12, 8, 9, 3, 7, 11)
    main = jnp.concatenate([sec(o) for o in order] + [w[:, merge0:]], axis=1)
    small = jnp.pad(w[:, small0:merge0], ((0, 0), (0, LANE - (merge0 - small0))))
    values_t = jnp.concatenate([sec(2), sec(6), sec(10)], axis=1).T
    return main.astype(bf16), small.astype(bf16), values_t.astype(bf16)


def _rope_tables(seq):
    half = ROPE_DIM // 2
    inv_freq = jnp.power(jnp.float32(ROPE_THETA), -jnp.arange(0, ROPE_DIM, 2, dtype=f32) / ROPE_DIM)
    ang = jnp.arange(seq).astype(f32)[:, None] * inv_freq[None, :]
    cos, sin = jnp.cos(ang), jnp.sin(ang)
    pad = jnp.zeros((seq, HEAD_DIM - ROPE_DIM), f32)
    zero = jnp.zeros((seq, half), f32)
    tile = lambda a: jnp.tile(a, (1, LANE // HEAD_DIM))
    cos_t = tile(jnp.concatenate([cos, cos, pad + 1.0], axis=1))
    sin_a = tile(jnp.concatenate([-sin, zero, pad], axis=1))
    sin_b = tile(jnp.concatenate([zero, sin, pad], axis=1))
    return cos_t, sin_a, sin_b


def kernel(x, norm_gain, w_in, forget_bias, w_branch, w_out, final_gain):
    b, s, d_model = x.shape
    depth = norm_gain.shape[0]
    assert s % TQ == 0 and s % min(PROJ_TM, s) == 0 and s % min(CSUM_CHUNK, s) == 0
    cos_t, sin_a, sin_b = _rope_tables(s)
    h2d = x.reshape(b * s, d_model)
    for layer in range(depth):
        w_main, w_small, w_vt = _pack_w_in(w_in[layer])
        gain = norm_gain[layer][None, :]
        fbias = jnp.zeros((1, LANE), f32).at[0, SMALL_F:SMALL_F + N_HEADS].set(forget_bias[layer])
        p2d, small2d = _project(h2d, gain, w_main, w_small, cos_t, sin_a, sin_b, fbias, s)
        vt = _project_values(h2d, gain, w_vt, s)
        p3d = p2d.reshape(b, s, N_SEC * WIDTH)
        small3d = small2d.reshape(b, s, LANE)
        extras = _cumsum_extras(small3d)
        k_idx = small3d[:, :, SMALL_KIDX:SMALL_KIDX + IDX_DIM].astype(bf16)
        kk = jnp.concatenate([k_idx, k_idx], axis=-1)
        oa = _moba(p3d, vt)
        ob = _dsa(p3d, vt, kk, small3d)
        oc = _fox(p3d, extras, vt)
        h2d = _output(oa.reshape(b * s, WIDTH), ob.reshape(b * s, WIDTH), oc.reshape(b * s, WIDTH),
                      p2d, h2d, w_branch[layer].astype(bf16), w_out[layer].astype(bf16),
                      final_gain[None, :], layer == depth - 1)
    return h2d.reshape(b, s, d_model)
```

```python
import functools

import jax
import jax.numpy as jnp
from jax import lax
from jax.experimental import pallas as pl
from jax.experimental.pallas import tpu as pltpu

HEAD_DIM = 64
N_HEADS = 8
WIDTH = N_HEADS * HEAD_DIM
N_BRANCH = 3
ROPE_DIM = HEAD_DIM // 4
ROPE_THETA = 500000.0
MOBA_BLOCK = 256
MOBA_TOPK = 3
IDX_HEADS = 8
IDX_DIM = 64
DSA_TOPK_MAX = 256
RMS_EPS = 1e-6
LOG2E = 1.4426950408889634
ATTN_SCALE = HEAD_DIM ** -0.5 * LOG2E
IDX_SCALE = IDX_HEADS ** -0.5 * IDX_DIM ** -0.5

LANE = 128
SUBLANE = 8
BF16_ROWS = 16
VMEM_LIMIT_BYTES = 48 * 1024 * 1024

TQ = 256
TK = 256
PROJ_TM = 2048
PROJ_SUB = 512
OUT_TM = 512
CSUM_CHUNK = 512
ACC_ROWS = HEAD_DIM + BF16_ROWS

NEG = -1e30
INT_MIN = -(2 ** 31)
COUNT_CHAINS = 8

SEC_QA, SEC_KA, SEC_QB, SEC_KB, SEC_QIDX = 0, 1, 2, 3, 4
SEC_QC, SEC_KC = 5, 6
SEC_GA, SEC_GB, SEC_GC = 7, 8, 9
SEC_MERGE = 10
N_SEC = 16
SMALL_KIDX, SMALL_WIDX, SMALL_F = 0, 64, 72
N_PIECES = 3

f32 = jnp.float32
bf16 = jnp.bfloat16


def _nt_dot(a, b):
    return lax.dot_general(a, b, (((1,), (1,)), ((), ())), preferred_element_type=f32)


def _split3(x):
    hi = x.astype(bf16)
    r1 = x - hi.astype(f32)
    mid = r1.astype(bf16)
    lo = (r1 - mid.astype(f32)).astype(bf16)
    return hi, mid, lo


def _rope(x, cos, sin_a, sin_b):
    return x * cos + pltpu.roll(x, LANE - ROPE_DIM // 2, 1) * sin_a + pltpu.roll(x, ROPE_DIM // 2, 1) * sin_b


def _rms_norm_bf16(x_ref, gain_ref):
    xf = x_ref[...]
    ms = jnp.mean(xf * xf, axis=-1, keepdims=True)
    return (xf * lax.rsqrt(ms + RMS_EPS) * gain_ref[...]).astype(bf16)


def _proj_kernel(x_ref, gain_ref, w_ref, ws_ref, cos_ref, sa_ref, sb_ref, fb_ref,
                 p_ref, small_ref, h_ref):
    j = pl.program_id(1)

    @pl.when(j == 0)
    def _():
        h_ref[...] = _rms_norm_bf16(x_ref, gain_ref)
        sm = jnp.dot(h_ref[...], ws_ref[...], preferred_element_type=f32)
        lane = lax.broadcasted_iota(jnp.int32, sm.shape, 1)
        z = sm + fb_ref[...]
        log_f = jnp.minimum(z, 0.0) - jnp.log1p(jnp.exp(-jnp.abs(z)))
        small_ref[...] = jnp.where(lane < SMALL_WIDX, _rope(sm, cos_ref[...], sa_ref[...], sb_ref[...]),
                                   jnp.where(lane < SMALL_F, sm * IDX_SCALE, log_f))

    def row_blocks(epilogue):
        for r in range(h_ref.shape[0] // PROJ_SUB):
            rows = slice(r * PROJ_SUB, (r + 1) * PROJ_SUB)
            acc = jnp.dot(h_ref[rows, :], w_ref[...], preferred_element_type=f32)
            p_ref[rows, :] = epilogue(acc, rows).astype(bf16)

    def rope_all(acc, rows):
        return jnp.concatenate(
            [_rope(acc[:, c * LANE:(c + 1) * LANE], cos_ref[rows, :], sa_ref[rows, :], sb_ref[rows, :])
             for c in range(WIDTH // LANE)], axis=1)

    @pl.when(j <= SEC_QIDX)
    def _():
        scale = jnp.where((j == SEC_QA) | (j == SEC_QB), ATTN_SCALE, 1.0).astype(f32)
        row_blocks(lambda acc, rows: rope_all(acc, rows) * scale)

    @pl.when((j > SEC_QIDX) & (j < SEC_GA))
    def _():
        scale = jnp.where(j == SEC_QC, ATTN_SCALE, 1.0).astype(f32)
        row_blocks(lambda acc, rows: acc * scale)

    @pl.when((j >= SEC_GA) & (j < SEC_MERGE))
    def _():
        row_blocks(lambda acc, rows: acc * jax.nn.sigmoid(acc))

    @pl.when(j >= SEC_MERGE)
    def _():
        row_blocks(lambda acc, rows: jax.nn.sigmoid(acc))


def _project(x2d, gain, w_main, w_small, cos, sin_a, sin_b, fbias, seq):
    n_rows, d_model = x2d.shape
    tm = min(PROJ_TM, seq)
    n_seq_tiles = seq // tm
    return pl.pallas_call(
        _proj_kernel,
        out_shape=(jax.ShapeDtypeStruct((n_rows, N_SEC * WIDTH), bf16),
                   jax.ShapeDtypeStruct((n_rows, LANE), f32)),
        grid=(n_rows // tm, N_SEC),
        in_specs=[
            pl.BlockSpec((tm, d_model), lambda i, j: (i, 0)),
            pl.BlockSpec((1, d_model), lambda i, j: (0, 0)),
            pl.BlockSpec((d_model, WIDTH), lambda i, j: (0, j)),
            pl.BlockSpec((d_model, LANE), lambda i, j: (0, 0)),
            pl.BlockSpec((tm, LANE), lambda i, j: (i % n_seq_tiles, 0)),
            pl.BlockSpec((tm, LANE), lambda i, j: (i % n_seq_tiles, 0)),
            pl.BlockSpec((tm, LANE), lambda i, j: (i % n_seq_tiles, 0)),
            pl.BlockSpec((1, LANE), lambda i, j: (0, 0)),
        ],
        out_specs=(pl.BlockSpec((tm, WIDTH), lambda i, j: (i, j)),
                   pl.BlockSpec((tm, LANE), lambda i, j: (i, 0))),
        scratch_shapes=[pltpu.VMEM((tm, d_model), bf16)],
        compiler_params=pltpu.CompilerParams(
            dimension_semantics=("arbitrary", "arbitrary"), vmem_limit_bytes=VMEM_LIMIT_BYTES),
        name="proj",
    )(x2d, gain, w_main, w_small, cos, sin_a, sin_b, fbias)


def _vproj_kernel(x_ref, gain_ref, wt_ref, vt_ref, h_ref):
    @pl.when(pl.program_id(1) == 0)
    def _():
        h_ref[...] = _rms_norm_bf16(x_ref, gain_ref)

    vt_ref[...] = _nt_dot(wt_ref[...], h_ref[...]).astype(bf16)


def _project_values(x2d, gain, w_vt, seq):
    n_rows, d_model = x2d.shape
    tm = min(PROJ_TM, seq)
    return pl.pallas_call(
        _vproj_kernel,
        out_shape=jax.ShapeDtypeStruct((N_BRANCH * WIDTH, n_rows), bf16),
        grid=(n_rows // tm, N_BRANCH),
        in_specs=[
            pl.BlockSpec((tm, d_model), lambda i, j: (i, 0)),
            pl.BlockSpec((1, d_model), lambda i, j: (0, 0)),
            pl.BlockSpec((WIDTH, d_model), lambda i, j: (j, 0)),
        ],
        out_specs=pl.BlockSpec((WIDTH, tm), lambda i, j: (j, i)),
        scratch_shapes=[pltpu.VMEM((tm, d_model), bf16)],
        compiler_params=pltpu.CompilerParams(
            dimension_semantics=("arbitrary", "arbitrary"), vmem_limit_bytes=VMEM_LIMIT_BYTES),
        name="vproj",
    )(x2d, gain, w_vt)


def _cumsum_kernel(x_ref, e_ref, carry_ref):
    @pl.when(pl.program_id(1) == 0)
    def _():
        carry_ref[...] = jnp.zeros_like(carry_ref)

    x = x_ref[...]
    ch = x.shape[0]
    tri = (lax.broadcasted_iota(jnp.int32, (ch, ch), 1)
           <= lax.broadcasted_iota(jnp.int32, (ch, ch), 0)).astype(bf16)
    cs = sum(jnp.dot(tri, piece, preferred_element_type=f32) for piece in _split3(x)) + carry_ref[0:1, :]
    carry_ref[...] = jnp.broadcast_to(cs[ch - 1:ch, :], carry_ref.shape)

    src = lax.broadcasted_iota(jnp.int32, (LANE, WIDTH), 0) - SMALL_F
    dst = lax.broadcasted_iota(jnp.int32, (LANE, WIDTH), 1)
    is_head = (src >= 0) & (src < N_HEADS)
    base = (src >> 1) * LANE + (src & 1) * N_PIECES
    out = None
    for n, piece in enumerate(_split3(cs * -LOG2E)):
        place = jnp.where(is_head & (dst == base + n), 1.0, 0.0).astype(bf16)
        term = jnp.dot(piece, place, preferred_element_type=f32)
        out = term if out is None else out + term
    e_ref[...] = out.astype(bf16)


def _cumsum_extras(small3d):
    b, s, _ = small3d.shape
    ch = min(CSUM_CHUNK, s)
    return pl.pallas_call(
        _cumsum_kernel,
        out_shape=jax.ShapeDtypeStruct((b, s, WIDTH), bf16),
        grid=(b, s // ch),
        in_specs=[pl.BlockSpec((None, ch, LANE), lambda i, c: (i, c, 0))],
        out_specs=pl.BlockSpec((None, ch, WIDTH), lambda i, c: (i, c, 0)),
        scratch_shapes=[pltpu.VMEM((SUBLANE, LANE), f32)],
        compiler_params=pltpu.CompilerParams(
            dimension_semantics=("arbitrary", "arbitrary"), vmem_limit_bytes=VMEM_LIMIT_BYTES),
        name="csum",
    )(small3d)


def _head_mask(q2, h):
    lane = lax.broadcasted_iota(jnp.int32, q2.shape, 1)
    return jnp.where((lane >= HEAD_DIM * h) & (lane < HEAD_DIM * (h + 1)), q2, jnp.zeros_like(q2))


def _slab(h):
    return slice((h // 2) * LANE, (h // 2 + 1) * LANE)


def _queries_t(q_ref):
    row = lax.broadcasted_iota(jnp.int32, (LANE, TQ), 0)
    out = []
    for h in range(N_HEADS):
        if h % 2 == 0:
            qt = q_ref[:, _slab(h)].astype(f32).T
        first = HEAD_DIM * (h % 2)
        out.append(jnp.where((row >= first) & (row < first + HEAD_DIM), qt, 0.0).astype(bf16))
    return out


def _values_with_ones(vt_ref, h, off):
    return jnp.concatenate([vt_ref[h * HEAD_DIM:(h + 1) * HEAD_DIM, pl.ds(off, TK)],
                            jnp.ones((BF16_ROWS, TK), bf16)], axis=0)


def _online_update(s, m, vt, acc_ref, h, picked=None):
    mx = jnp.max(s, axis=0, keepdims=True)
    if picked is not None:
        mx = jnp.where(picked, mx, NEG)
    m_new = jnp.maximum(m, mx)
    shift = m_new if picked is None else jnp.where(picked, m_new, -NEG)
    p = jnp.exp2(s - shift).astype(bf16)
    acc_ref[h] = jnp.exp2(m - m_new) * acc_ref[h] + jnp.dot(vt, p, preferred_element_type=f32)
    return m_new


def _write_output(acc_ref, o_ref):
    for pair in range(N_HEADS // 2):
        halves = []
        for h in (2 * pair, 2 * pair + 1):
            a = acc_ref[h]
            halves.append(a[:HEAD_DIM] / a[HEAD_DIM:HEAD_DIM + 1])
        o_ref[:, pair * LANE:(pair + 1) * LANE] = jnp.concatenate(halves, axis=0).T.astype(bf16)


def _diag_keep():
    return lax.broadcasted_iota(jnp.int32, (TK, TQ), 0) <= lax.broadcasted_iota(jnp.int32, (TK, TQ), 1)


def _init_rows():
    return tuple(jnp.full((1, TQ), NEG, f32) for _ in range(N_HEADS))


def _for_key_tiles(n, scores, update, carry):
    def pair(i, c):
        j = 2 * i
        s_a, s_b = scores(j), scores(j + 1)
        return update(j + 1, s_b, update(j, s_a, c))

    carry = lax.fori_loop(0, lax.shift_right_logical(n, 1), pair, carry)
    return lax.cond((n & 1) == 1, lambda c: update(n - 1, scores(n - 1), c), lambda c: c, carry)


def _fox_kernel(q_ref, k_ref, e_ref, vt_ref, o_ref, acc_ref):
    t = pl.program_id(1)
    d0 = pl.multiple_of(t * TQ, TQ)
    acc_ref[...] = jnp.zeros_like(acc_ref)
    keep = _diag_keep()
    row = lax.broadcasted_iota(jnp.int32, (LANE, TQ), 0)
    q_aug = []
    for h, qt in enumerate(_queries_t(q_ref)):
        first = (h % 2) * N_PIECES
        ones = jnp.where((row >= first) & (row < first + N_PIECES), 1.0, 0.0).astype(bf16)
        q_aug.append(jnp.concatenate([qt, ones], axis=0))

    def scores(j):
        off = pl.multiple_of(j * TK, TK)
        out = []
        for h in range(N_HEADS):
            if h % 2 == 0:
                k_aug = jnp.concatenate([k_ref[pl.ds(off, TK), _slab(h)], e_ref[pl.ds(off, TK), _slab(h)]], axis=1)
            out.append(jnp.dot(k_aug, q_aug[h], preferred_element_type=f32))
        return out

    def update(j, sc, ms, diagonal=False):
        off = pl.multiple_of(j * TK, TK)
        out = []
        for h, s in enumerate(sc):
            if diagonal:
                s = jnp.where(keep, s, NEG)
            out.append(_online_update(s, ms[h], _values_with_ones(vt_ref, h, off), acc_ref, h))
        return tuple(out)

    ms = update(t, scores(t), _init_rows(), diagonal=True)
    _for_key_tiles(t, scores, update, ms)
    _write_output(acc_ref, o_ref)


def _fox(p3d, extras, vt):
    b, s, _ = p3d.shape
    return pl.pallas_call(
        _fox_kernel,
        out_shape=jax.ShapeDtypeStruct((b, s, WIDTH), bf16),
        grid=(b, s // TQ),
        in_specs=[
            pl.BlockSpec((None, TQ, WIDTH), lambda i, t: (i, t, SEC_QC)),
            pl.BlockSpec((None, s, WIDTH), lambda i, t: (i, 0, SEC_KC)),
            pl.BlockSpec((None, s, WIDTH), lambda i, t: (i, 0, 0)),
            pl.BlockSpec((WIDTH, s), lambda i, t: (2, i)),
        ],
        out_specs=pl.BlockSpec((None, TQ, WIDTH), lambda i, t: (i, t, 0)),
        scratch_shapes=[pltpu.VMEM((N_HEADS, ACC_ROWS, TQ), f32)],
        compiler_params=pltpu.CompilerParams(
            dimension_semantics=("arbitrary", "arbitrary"), vmem_limit_bytes=VMEM_LIMIT_BYTES),
        name="fox",
    )(p3d, p3d, extras, vt)


def _moba_kernel(q_ref, k_ref, vt_ref, o_ref, acc_ref, km_ref, sel_ref, *, n_blk):
    t = pl.program_id(1)
    nb_pad = km_ref.shape[0]

    @pl.when(t == 0)
    def _():
        km_ref[...] = jnp.zeros_like(km_ref)
        for blk in range(n_blk):
            kb = k_ref[blk * MOBA_BLOCK:(blk + 1) * MOBA_BLOCK, :].astype(f32)
            km_ref[blk:blk + 1, :] = jnp.sum(kb, axis=0, keepdims=True) * (1.0 / MOBA_BLOCK)

    d0 = pl.multiple_of(t * TQ, TQ)
    acc_ref[...] = jnp.zeros_like(acc_ref)
    keep = _diag_keep()
    blk_row = lax.broadcasted_iota(jnp.int32, (nb_pad, TQ), 0)
    blk_row_f = blk_row.astype(f32)
    qt = _queries_t(q_ref)

    for h in range(N_HEADS):
        gate = jnp.dot(km_ref[:, _slab(h)].astype(bf16), qt[h], preferred_element_type=f32)
        g = jnp.where(blk_row < t, gate, NEG)
        sel = jnp.zeros((nb_pad, TQ), f32)
        for _ in range(min(MOBA_TOPK, n_blk - 1)):
            mx = jnp.max(g, axis=0, keepdims=True)
            first = jnp.min(jnp.where(g == mx, blk_row_f, float(nb_pad)), axis=0, keepdims=True)
            hit = blk_row_f == first
            sel = jnp.where(hit, jnp.where(mx > 0.5 * NEG, 1.0, sel), sel)
            g = jnp.where(hit, NEG, g)
        sel_ref[h] = sel

    def scores(j):
        off = pl.multiple_of(j * TK, TK)
        return [jnp.dot(k_ref[pl.ds(off, TK), _slab(h)], qt[h], preferred_element_type=f32)
                for h in range(N_HEADS)]

    def update(j, sc, ms, diagonal=False):
        off = pl.multiple_of(j * TK, TK)
        out = []
        for h, s in enumerate(sc):
            if diagonal:
                s, picked = jnp.where(keep, s, NEG), None
            else:
                picked = sel_ref[h, pl.ds(j, 1), :] > 0.5
            out.append(_online_update(s, ms[h], _values_with_ones(vt_ref, h, off), acc_ref, h, picked))
        return tuple(out)

    ms = update(t, scores(t), _init_rows(), diagonal=True)
    _for_key_tiles(t, scores, update, ms)
    _write_output(acc_ref, o_ref)


def _moba(p3d, vt):
    b, s, _ = p3d.shape
    assert TQ == MOBA_BLOCK and TK == MOBA_BLOCK and s % MOBA_BLOCK == 0
    n_blk = s // MOBA_BLOCK
    nb_pad = -(-n_blk // BF16_ROWS) * BF16_ROWS
    return pl.pallas_call(
        functools.partial(_moba_kernel, n_blk=n_blk),
        out_shape=jax.ShapeDtypeStruct((b, s, WIDTH), bf16),
        grid=(b, n_blk),
        in_specs=[
            pl.BlockSpec((None, TQ, WIDTH), lambda i, t: (i, t, SEC_QA)),
            pl.BlockSpec((None, s, WIDTH), lambda i, t: (i, 0, SEC_KA)),
            pl.BlockSpec((WIDTH, s), lambda i, t: (0, i)),
        ],
        out_specs=pl.BlockSpec((None, TQ, WIDTH), lambda i, t: (i, t, 0)),
        scratch_shapes=[pltpu.VMEM((N_HEADS, ACC_ROWS, TQ), f32),
                        pltpu.VMEM((nb_pad, WIDTH), f32),
                        pltpu.VMEM((N_HEADS, nb_pad, TQ), f32)],
        compiler_params=pltpu.CompilerParams(
            dimension_semantics=("arbitrary", "arbitrary"), vmem_limit_bytes=VMEM_LIMIT_BYTES),
        name="moba",
    )(p3d, p3d, vt)


def _dsa_kernel(q_ref, k_ref, vt_ref, qi_ref, kk_ref, sm_ref, o_ref, acc_ref, sc_ref, *, top, seq):
    t = pl.program_id(1)
    n_kv = t + 1
    n_chunks = lax.shift_right_logical(n_kv + 1, 1)
    k_top = float(top)
    keep = _diag_keep()
    chunk_row = lax.broadcasted_iota(jnp.int32, (2 * TK, TQ), 0)
    group_row = lax.broadcasted_iota(jnp.int32, (SUBLANE, TQ), 0)
    w_rows = sm_ref[...].T
    qit = _queries_t(qi_ref)

    @pl.when((n_kv & 1) == 1)
    def _():
        sc_ref[pl.ds(pl.multiple_of(n_kv * TK, TK), TK), :] = jnp.full((TK, TQ), -jnp.inf, f32)

    def score_body(j, _):
        off = pl.multiple_of(j * TK, TK)
        kkb = kk_ref[pl.ds(off, TK), :]
        rel = [jnp.dot(kkb, qit[hh], preferred_element_type=f32) for hh in range(IDX_HEADS)]
        isc = jnp.zeros((TK, TQ), f32)
        for hh in range(IDX_HEADS):
            isc = isc + w_rows[SMALL_WIDX + hh:SMALL_WIDX + hh + 1, :] * jnp.maximum(rel[hh], 0.0)
        sc_ref[pl.ds(off, TK), :] = jnp.where((j < t) | keep, isc, -jnp.inf)
        return 0

    lax.fori_loop(0, n_kv, score_body, 0)

    def count(pred):
        def body(c, accs):
            off = pl.multiple_of(c * (2 * TK), 2 * TK)
            chunk = sc_ref[pl.ds(off, 2 * TK), :]
            accs = list(accs)
            for g in range(2 * TK // SUBLANE):
                rows = chunk[g * SUBLANE:(g + 1) * SUBLANE, :]
                hit = jnp.where(pred(rows, off + g * SUBLANE), 1.0, 0.0)
                accs[g % COUNT_CHAINS] = accs[g % COUNT_CHAINS] + hit
            return tuple(accs)
        accs = lax.fori_loop(0, n_chunks, body, tuple(jnp.zeros((SUBLANE, TQ), f32) for _ in range(COUNT_CHAINS)))
        return jnp.sum(sum(accs), axis=0, keepdims=True)

    def as_score(code):
        return pltpu.bitcast(jnp.where(code < 0, code ^ 0x7FFFFFFF, code), f32)

    c_pos = count(lambda sc, off: sc >= 0.0)
    code = jnp.where(c_pos >= k_top, 0, INT_MIN).astype(jnp.int32)
    n_ge = jnp.where(c_pos >= k_top, c_pos, (n_kv * TK).astype(f32))

    def bit_body(i, carry):
        code, n_ge = carry
        cand = code + jnp.left_shift(jnp.int32(1), 30 - i)
        cand_f = as_score(cand)
        c = count(lambda sc, off: sc >= cand_f)
        ok = (c >= k_top) & (cand_f > -jnp.inf)
        return jnp.where(ok, cand, code), jnp.where(ok, c, n_ge)

    code, n_ge = lax.fori_loop(0, 31, bit_body, (code, n_ge))
    has_thr = code > INT_MIN
    thr = jnp.where(has_thr, as_score(code), -jnp.inf)

    no_cut = jnp.where(has_thr, seq + 1, 0).astype(jnp.int32)
    over = jnp.max(jnp.where(has_thr & (n_ge > k_top), 1.0, 0.0))

    def tie_cut():
        need = k_top - count(lambda sc, off: sc > thr)
        n_bits = seq.bit_length()

        def cut_body(i, cut):
            cand = cut + jnp.left_shift(jnp.int32(1), (n_bits - 1) - i)
            c = count(lambda sc, first: (sc == thr) & (group_row + first < cand))
            return jnp.where(c <= need, cand, cut)

        cut = lax.fori_loop(0, n_bits, cut_body, jnp.zeros((1, TQ), jnp.int32))
        return jnp.where(has_thr, cut, 0)

    cut = lax.cond(over > 0.5, tie_cut, lambda: no_cut)

    def ceiling_body(c, _):
        off = pl.multiple_of(c * (2 * TK), 2 * TK)
        sc = sc_ref[pl.ds(off, 2 * TK), :]
        chosen = (sc > thr) | ((sc == thr) & (chunk_row + off < cut))
        sc_ref[pl.ds(off, 2 * TK), :] = jnp.where(chosen, -NEG, NEG).astype(f32)
        return 0

    lax.fori_loop(0, n_chunks, ceiling_body, 0)

    acc_ref[...] = jnp.zeros_like(acc_ref)
    qt = _queries_t(q_ref)

    def scores(j):
        off = pl.multiple_of(j * TK, TK)
        return [jnp.dot(k_ref[pl.ds(off, TK), _slab(h)], qt[h], preferred_element_type=f32)
                for h in range(N_HEADS)]

    def update(j, sc, ms):
        off = pl.multiple_of(j * TK, TK)
        ceiling = sc_ref[pl.ds(off, TK), :]
        out = []
        for h, s in enumerate(sc):
            s = jnp.minimum(s, ceiling)
            out.append(_online_update(s, ms[h], _values_with_ones(vt_ref, h, off), acc_ref, h))
        return tuple(out)

    _for_key_tiles(n_kv, scores, update, _init_rows())
    _write_output(acc_ref, o_ref)


def _dsa(p3d, vt, kk, small3d):
    b, s, _ = p3d.shape
    top = min(DSA_TOPK_MAX, s // 4)
    return pl.pallas_call(
        functools.partial(_dsa_kernel, top=top, seq=s),
        out_shape=jax.ShapeDtypeStruct((b, s, WIDTH), bf16),
        grid=(b, s // TQ),
        in_specs=[
            pl.BlockSpec((None, TQ, WIDTH), lambda i, t: (i, t, SEC_QB)),
            pl.BlockSpec((None, s, WIDTH), lambda i, t: (i, 0, SEC_KB)),
            pl.BlockSpec((WIDTH, s), lambda i, t: (1, i)),
            pl.BlockSpec((None, TQ, WIDTH), lambda i, t: (i, t, SEC_QIDX)),
            pl.BlockSpec((None, s, LANE), lambda i, t: (i, 0, 0)),
            pl.BlockSpec((None, TQ, LANE), lambda i, t: (i, t, 0)),
        ],
        out_specs=pl.BlockSpec((None, TQ, WIDTH), lambda i, t: (i, t, 0)),
        scratch_shapes=[pltpu.VMEM((N_HEADS, ACC_ROWS, TQ), f32), pltpu.VMEM((s, TQ), f32)],
        compiler_params=pltpu.CompilerParams(
            dimension_semantics=("arbitrary", "arbitrary"), vmem_limit_bytes=VMEM_LIMIT_BYTES),
        name="dsa",
    )(p3d, p3d, vt, p3d, kk, small3d)


def _out_kernel(oa_ref, ob_ref, oc_ref, ga_ref, gb_ref, gc_ref, m0_ref, m1_ref, m2_ref,
                x_ref, wb_ref, wo_ref, fg_ref, y_ref, *, final_norm):
    merged = None
    for n, (o_ref, g_ref, mg_ref) in enumerate(((oa_ref, ga_ref, m0_ref), (ob_ref, gb_ref, m1_ref),
                                                (oc_ref, gc_ref, m2_ref))):
        gated = (o_ref[...].astype(f32) * g_ref[...].astype(f32)).astype(bf16)
        y = jnp.dot(gated, wb_ref[n], preferred_element_type=f32) * mg_ref[...].astype(f32)
        merged = y if merged is None else merged + y
    out = x_ref[...] + jnp.dot(merged.astype(bf16), wo_ref[...], preferred_element_type=f32)
    if final_norm:
        ms = jnp.mean(out * out, axis=-1, keepdims=True)
        out = out * lax.rsqrt(ms + RMS_EPS) * fg_ref[...]
    y_ref[...] = out


def _output(oa, ob, oc, p2d, x2d, w_branch, w_out, final_gain, final_norm):
    n_rows, d_model = x2d.shape
    tm = min(OUT_TM, n_rows)
    assert SEC_MERGE * WIDTH % d_model == 0
    mg0 = SEC_MERGE * WIDTH // d_model
    row = lambda c: (lambda i: (i, c))
    return pl.pallas_call(
        functools.partial(_out_kernel, final_norm=final_norm),
        out_shape=jax.ShapeDtypeStruct((n_rows, d_model), f32),
        grid=(n_rows // tm,),
        in_specs=[
            pl.BlockSpec((tm, WIDTH), row(0)), pl.BlockSpec((tm, WIDTH), row(0)), pl.BlockSpec((tm, WIDTH), row(0)),
            pl.BlockSpec((tm, WIDTH), row(SEC_GA)), pl.BlockSpec((tm, WIDTH), row(SEC_GB)),
            pl.BlockSpec((tm, WIDTH), row(SEC_GC)),
            pl.BlockSpec((tm, d_model), row(mg0)), pl.BlockSpec((tm, d_model), row(mg0 + 1)),
            pl.BlockSpec((tm, d_model), row(mg0 + 2)),
            pl.BlockSpec((tm, d_model), row(0)),
            pl.BlockSpec((N_BRANCH, WIDTH, d_model), lambda i: (0, 0, 0)),
            pl.BlockSpec((d_model, d_model), lambda i: (0, 0)),
            pl.BlockSpec((1, d_model), lambda i: (0, 0)),
        ],
        out_specs=pl.BlockSpec((tm, d_model), row(0)),
        compiler_params=pltpu.CompilerParams(
            dimension_semantics=("arbitrary",), vmem_limit_bytes=VMEM_LIMIT_BYTES),
        name="out",
    )(oa, ob, oc, p2d, p2d, p2d, p2d, p2d, p2d, x2d, w_branch, w_out, final_gain)


def _pack_w_in(w):
    sec = lambda o: w[:, o * WIDTH:(o + 1) * WIDTH]
    small0 = 12 * WIDTH + IDX_HEADS * IDX_DIM
    merge0 = small0 + IDX_DIM + IDX_HEADS + N_HEADS
    order = (0, 1, 4, 5, 12, 8, 9, 3, 7, 11)
    main = jnp.concatenate([sec(o) for o in order] + [w[:, merge0:]], axis=1)
    small = jnp.pad(w[:, small0:merge0], ((0, 0), (0, LANE - (merge0 - small0))))
    values_t = jnp.concatenate([sec(2), sec(6), sec(10)], axis=1).T
    return main.astype(bf16), small.astype(bf16), values_t.astype(bf16)


def _rope_tables(seq):
    half = ROPE_DIM // 2
    inv_freq = jnp.power(jnp.float32(ROPE_THETA), -jnp.arange(0, ROPE_DIM, 2, dtype=f32) / ROPE_DIM)
    ang = jnp.arange(seq).astype(f32)[:, None] * inv_freq[None, :]
    cos, sin = jnp.cos(ang), jnp.sin(ang)
    pad = jnp.zeros((seq, HEAD_DIM - ROPE_DIM), f32)
    zero = jnp.zeros((seq, half), f32)
    tile = lambda a: jnp.tile(a, (1, LANE // HEAD_DIM))
    cos_t = tile(jnp.concatenate([cos, cos, pad + 1.0], axis=1))
    sin_a = tile(jnp.concatenate([-sin, zero, pad], axis=1))
    sin_b = tile(jnp.concatenate([zero, sin, pad], axis=1))
    return cos_t, sin_a, sin_b


def kernel(x, norm_gain, w_in, forget_bias, w_branch, w_out, final_gain):
    b, s, d_model = x.shape
    depth = norm_gain.shape[0]
    assert s % TQ == 0 and s % min(PROJ_TM, s) == 0 and s % min(CSUM_CHUNK, s) == 0
    cos_t, sin_a, sin_b = _rope_tables(s)
    h2d = x.reshape(b * s, d_model)
    for layer in range(depth):
        w_main, w_small, w_vt = _pack_w_in(w_in[layer])
        gain = norm_gain[layer][None, :]
        fbias = jnp.zeros((1, LANE), f32).at[0, SMALL_F:SMALL_F + N_HEADS].set(forget_bias[layer])
        p2d, small2d = _project(h2d, gain, w_main, w_small, cos_t, sin_a, sin_b, fbias, s)
        vt = _project_values(h2d, gain, w_vt, s)
        p3d = p2d.reshape(b, s, N_SEC * WIDTH)
        small3d = small2d.reshape(b, s, LANE)
        extras = _cumsum_extras(small3d)
        k_idx = small3d[:, :, SMALL_KIDX:SMALL_KIDX + IDX_DIM].astype(bf16)
        kk = jnp.concatenate([k_idx, k_idx], axis=-1)
        oa = _moba(p3d, vt)
        ob = _dsa(p3d, vt, kk, small3d)
        oc = _fox(p3d, extras, vt)
        h2d = _output(oa.reshape(b * s, WIDTH), ob.reshape(b * s, WIDTH), oc.reshape(b * s, WIDTH),
                      p2d, h2d, w_branch[layer].astype(bf16), w_out[layer].astype(bf16),
                      final_gain[None, :], layer == depth - 1)
    return h2d.reshape(b, s, d_model)
```

```python
import functools

import jax
import jax.numpy as jnp
from jax import lax
from jax.experimental import pallas as pl
from jax.experimental.pallas import tpu as pltpu

HEAD_DIM = 64
N_HEADS = 8
WIDTH = N_HEADS * HEAD_DIM
N_BRANCH = 3
ROPE_DIM = HEAD_DIM // 4
ROPE_THETA = 500000.0
MOBA_BLOCK = 256
MOBA_TOPK = 3
IDX_HEADS = 8
IDX_DIM = 64
DSA_TOPK_MAX = 256
RMS_EPS = 1e-6
LOG2E = 1.4426950408889634
ATTN_SCALE = HEAD_DIM ** -0.5 * LOG2E
IDX_SCALE = IDX_HEADS ** -0.5 * IDX_DIM ** -0.5

LANE = 128
SUBLANE = 8
BF16_ROWS = 16
VMEM_LIMIT_BYTES = 48 * 1024 * 1024

TQ = 256
TK = 256
PROJ_TM = 2048
PROJ_SUB = 512
OUT_TM = 512
CSUM_CHUNK = 512
ACC_ROWS = HEAD_DIM + BF16_ROWS

NEG = -1e30
STALE_SHIFT_LIMIT = 64.0
INT_MIN = -(2 ** 31)
COUNT_CHAINS = 8

SEC_QA, SEC_KA, SEC_QB, SEC_KB, SEC_QIDX = 0, 1, 2, 3, 4
SEC_QC, SEC_KC = 5, 6
SEC_GA, SEC_GB, SEC_GC = 7, 8, 9
SEC_MERGE = 10
N_SEC = 16
SMALL_KIDX, SMALL_WIDX, SMALL_F = 0, 64, 72
N_PIECES = 3

f32 = jnp.float32
bf16 = jnp.bfloat16


def _nt_dot(a, b):
    return lax.dot_general(a, b, (((1,), (1,)), ((), ())), preferred_element_type=f32)


def _split3(x):
    hi = x.astype(bf16)
    r1 = x - hi.astype(f32)
    mid = r1.astype(bf16)
    lo = (r1 - mid.astype(f32)).astype(bf16)
    return hi, mid, lo


def _rope(x, cos, sin_a, sin_b):
    return x * cos + pltpu.roll(x, LANE - ROPE_DIM // 2, 1) * sin_a + pltpu.roll(x, ROPE_DIM // 2, 1) * sin_b


def _rms_norm_bf16(x_ref, gain_ref):
    xf = x_ref[...]
    ms = jnp.mean(xf * xf, axis=-1, keepdims=True)
    return (xf * lax.rsqrt(ms + RMS_EPS) * gain_ref[...]).astype(bf16)


def _proj_kernel(x_ref, gain_ref, w_ref, ws_ref, cos_ref, sa_ref, sb_ref, fb_ref,
                 p_ref, small_ref, h_ref):
    j = pl.program_id(1)

    @pl.when(j == 0)
    def _():
        h_ref[...] = _rms_norm_bf16(x_ref, gain_ref)
        sm = jnp.dot(h_ref[...], ws_ref[...], preferred_element_type=f32)
        lane = lax.broadcasted_iota(jnp.int32, sm.shape, 1)
        z = sm + fb_ref[...]
        log_f = jnp.minimum(z, 0.0) - jnp.log1p(jnp.exp(-jnp.abs(z)))
        small_ref[...] = jnp.where(lane < SMALL_WIDX, _rope(sm, cos_ref[...], sa_ref[...], sb_ref[...]),
                                   jnp.where(lane < SMALL_F, sm * IDX_SCALE, log_f))

    def row_blocks(epilogue):
        for r in range(h_ref.shape[0] // PROJ_SUB):
            rows = slice(r * PROJ_SUB, (r + 1) * PROJ_SUB)
            acc = jnp.dot(h_ref[rows, :], w_ref[...], preferred_element_type=f32)
            p_ref[rows, :] = epilogue(acc, rows).astype(bf16)

    def rope_all(acc, rows):
        return jnp.concatenate(
            [_rope(acc[:, c * LANE:(c + 1) * LANE], cos_ref[rows, :], sa_ref[rows, :], sb_ref[rows, :])
             for c in range(WIDTH // LANE)], axis=1)

    @pl.when(j <= SEC_QIDX)
    def _():
        scale = jnp.where((j == SEC_QA) | (j == SEC_QB), ATTN_SCALE, 1.0).astype(f32)
        row_blocks(lambda acc, rows: rope_all(acc, rows) * scale)

    @pl.when((j > SEC_QIDX) & (j < SEC_GA))
    def _():
        scale = jnp.where(j == SEC_QC, ATTN_SCALE, 1.0).astype(f32)
        row_blocks(lambda acc, rows: acc * scale)

    @pl.when((j >= SEC_GA) & (j < SEC_MERGE))
    def _():
        row_blocks(lambda acc, rows: acc * jax.nn.sigmoid(acc))

    @pl.when(j >= SEC_MERGE)
    def _():
        row_blocks(lambda acc, rows: jax.nn.sigmoid(acc))


def _project(x2d, gain, w_main, w_small, cos, sin_a, sin_b, fbias, seq):
    n_rows, d_model = x2d.shape
    tm = min(PROJ_TM, seq)
    n_seq_tiles = seq // tm
    return pl.pallas_call(
        _proj_kernel,
        out_shape=(jax.ShapeDtypeStruct((n_rows, N_SEC * WIDTH), bf16),
                   jax.ShapeDtypeStruct((n_rows, LANE), f32)),
        grid=(n_rows // tm, N_SEC),
        in_specs=[
            pl.BlockSpec((tm, d_model), lambda i, j: (i, 0)),
            pl.BlockSpec((1, d_model), lambda i, j: (0, 0)),
            pl.BlockSpec((d_model, WIDTH), lambda i, j: (0, j)),
            pl.BlockSpec((d_model, LANE), lambda i, j: (0, 0)),
            pl.BlockSpec((tm, LANE), lambda i, j: (i % n_seq_tiles, 0)),
            pl.BlockSpec((tm, LANE), lambda i, j: (i % n_seq_tiles, 0)),
            pl.BlockSpec((tm, LANE), lambda i, j: (i % n_seq_tiles, 0)),
            pl.BlockSpec((1, LANE), lambda i, j: (0, 0)),
        ],
        out_specs=(pl.BlockSpec((tm, WIDTH), lambda i, j: (i, j)),
                   pl.BlockSpec((tm, LANE), lambda i, j: (i, 0))),
        scratch_shapes=[pltpu.VMEM((tm, d_model), bf16)],
        compiler_params=pltpu.CompilerParams(
            dimension_semantics=("arbitrary", "arbitrary"), vmem_limit_bytes=VMEM_LIMIT_BYTES),
        name="proj",
    )(x2d, gain, w_main, w_small, cos, sin_a, sin_b, fbias)


def _vproj_kernel(x_ref, gain_ref, wt_ref, vt_ref, h_ref):
    @pl.when(pl.program_id(1) == 0)
    def _():
        h_ref[...] = _rms_norm_bf16(x_ref, gain_ref)

    vt_ref[...] = _nt_dot(wt_ref[...], h_ref[...]).astype(bf16)


def _project_values(x2d, gain, w_vt, seq):
    n_rows, d_model = x2d.shape
    tm = min(PROJ_TM, seq)
    return pl.pallas_call(
        _vproj_kernel,
        out_shape=jax.ShapeDtypeStruct((N_BRANCH * WIDTH, n_rows), bf16),
        grid=(n_rows // tm, N_BRANCH),
        in_specs=[
            pl.BlockSpec((tm, d_model), lambda i, j: (i, 0)),
            pl.BlockSpec((1, d_model), lambda i, j: (0, 0)),
            pl.BlockSpec((WIDTH, d_model), lambda i, j: (j, 0)),
        ],
        out_specs=pl.BlockSpec((WIDTH, tm), lambda i, j: (j, i)),
        scratch_shapes=[pltpu.VMEM((tm, d_model), bf16)],
        compiler_params=pltpu.CompilerParams(
            dimension_semantics=("arbitrary", "arbitrary"), vmem_limit_bytes=VMEM_LIMIT_BYTES),
        name="vproj",
    )(x2d, gain, w_vt)


def _cumsum_kernel(x_ref, e_ref, carry_ref):
    @pl.when(pl.program_id(1) == 0)
    def _():
        carry_ref[...] = jnp.zeros_like(carry_ref)

    x = x_ref[...]
    ch = x.shape[0]
    tri = (lax.broadcasted_iota(jnp.int32, (ch, ch), 1)
           <= lax.broadcasted_iota(jnp.int32, (ch, ch), 0)).astype(bf16)
    cs = sum(jnp.dot(tri, piece, preferred_element_type=f32) for piece in _split3(x)) + carry_ref[0:1, :]
    carry_ref[...] = jnp.broadcast_to(cs[ch - 1:ch, :], carry_ref.shape)

    src = lax.broadcasted_iota(jnp.int32, (LANE, WIDTH), 0) - SMALL_F
    dst = lax.broadcasted_iota(jnp.int32, (LANE, WIDTH), 1)
    is_head = (src >= 0) & (src < N_HEADS)
    base = (src >> 1) * LANE + (src & 1) * N_PIECES
    out = None
    for n, piece in enumerate(_split3(cs * -LOG2E)):
        place = jnp.where(is_head & (dst == base + n), 1.0, 0.0).astype(bf16)
        term = jnp.dot(piece, place, preferred_element_type=f32)
        out = term if out is None else out + term
    e_ref[...] = out.astype(bf16)


def _cumsum_extras(small3d):
    b, s, _ = small3d.shape
    ch = min(CSUM_CHUNK, s)
    return pl.pallas_call(
        _cumsum_kernel,
        out_shape=jax.ShapeDtypeStruct((b, s, WIDTH), bf16),
        grid=(b, s // ch),
        in_specs=[pl.BlockSpec((None, ch, LANE), lambda i, c: (i, c, 0))],
        out_specs=pl.BlockSpec((None, ch, WIDTH), lambda i, c: (i, c, 0)),
        scratch_shapes=[pltpu.VMEM((SUBLANE, LANE), f32)],
        compiler_params=pltpu.CompilerParams(
            dimension_semantics=("arbitrary", "arbitrary"), vmem_limit_bytes=VMEM_LIMIT_BYTES),
        name="csum",
    )(small3d)


def _head_mask(q2, h):
    lane = lax.broadcasted_iota(jnp.int32, q2.shape, 1)
    return jnp.where((lane >= HEAD_DIM * h) & (lane < HEAD_DIM * (h + 1)), q2, jnp.zeros_like(q2))


def _slab(h):
    return slice((h // 2) * LANE, (h // 2 + 1) * LANE)


def _queries_t(q_ref):
    row = lax.broadcasted_iota(jnp.int32, (LANE, TQ), 0)
    out = []
    for h in range(N_HEADS):
        if h % 2 == 0:
            qt = q_ref[:, _slab(h)].astype(f32).T
        first = HEAD_DIM * (h % 2)
        out.append(jnp.where((row >= first) & (row < first + HEAD_DIM), qt, 0.0).astype(bf16))
    return out


def _values_with_ones(vt_ref, h, off):
    return jnp.concatenate([vt_ref[h * HEAD_DIM:(h + 1) * HEAD_DIM, pl.ds(off, TK)],
                            jnp.ones((BF16_ROWS, TK), bf16)], axis=0)


def _online_update(s, m, vt, acc_ref, h, picked=None):
    mx = jnp.max(s, axis=0, keepdims=True)
    if picked is not None:
        mx = jnp.where(picked, mx, NEG)
    m_new = jnp.maximum(m, mx)
    shift = m_new if picked is None else jnp.where(picked, m_new, -NEG)
    p = jnp.exp2(s - shift).astype(bf16)
    acc_ref[h] = jnp.exp2(m - m_new) * acc_ref[h] + jnp.dot(vt, p, preferred_element_type=f32)
    return m_new


def _stale_shift_update(s, c, vt, acc_ref, h, picked=None):
    mx = jnp.max(s, axis=0, keepdims=True)
    if picked is not None:
        mx = jnp.where(picked, mx, NEG)
    shift = c if picked is None else jnp.where(picked, c, -NEG)
    p = jnp.exp2(s - shift).astype(bf16)
    c_new = jnp.maximum(c, mx)
    acc_ref[h] = jnp.exp2(c - c_new) * (acc_ref[h] + jnp.dot(vt, p, preferred_element_type=f32))
    return c_new, mx - c


def _write_output(acc_ref, o_ref):
    for pair in range(N_HEADS // 2):
        halves = []
        for h in (2 * pair, 2 * pair + 1):
            a = acc_ref[h]
            halves.append(a[:HEAD_DIM] / a[HEAD_DIM:HEAD_DIM + 1])
        o_ref[:, pair * LANE:(pair + 1) * LANE] = jnp.concatenate(halves, axis=0).T.astype(bf16)


def _diag_keep():
    return lax.broadcasted_iota(jnp.int32, (TK, TQ), 0) <= lax.broadcasted_iota(jnp.int32, (TK, TQ), 1)


def _init_rows():
    return tuple(jnp.full((1, TQ), NEG, f32) for _ in range(N_HEADS))


def _for_key_tiles(start, n, scores, update, carry):
    def pair(i, c):
        j = start + 2 * i
        s_a, s_b = scores(j), scores(j + 1)
        return update(j + 1, s_b, update(j, s_a, c))

    carry = lax.fori_loop(0, lax.shift_right_logical(n, 1), pair, carry)
    last = start + n - 1
    return lax.cond((n & 1) == 1, lambda c: update(last, scores(last), c), lambda c: c, carry)


def _make_updates(masked, vt_ref, acc_ref):
    def exact(j, sc, ms, diagonal=False):
        off = pl.multiple_of(j * TK, TK)
        out = []
        for h, s in enumerate(sc):
            s, picked = masked(j, h, s, diagonal)
            out.append(_online_update(s, ms[h], _values_with_ones(vt_ref, h, off), acc_ref, h, picked))
        return tuple(out)

    def one_pass(j, sc, carry):
        off = pl.multiple_of(j * TK, TK)
        rise, out = carry[N_HEADS], []
        for h, s in enumerate(sc):
            s, picked = masked(j, h, s, False)
            c_new, up = _stale_shift_update(s, carry[h], _values_with_ones(vt_ref, h, off), acc_ref, h, picked)
            out.append(c_new)
            rise = jnp.maximum(rise, up)
        return tuple(out) + (rise,)

    return exact, one_pass


def _attend(first, start, n, scores, masked, vt_ref, acc_ref, o_ref):
    exact, one_pass = _make_updates(masked, vt_ref, acc_ref)

    def run(update, tail):
        acc_ref[...] = jnp.zeros_like(acc_ref)
        ms = exact(first, scores(first), _init_rows(), diagonal=True)
        return _for_key_tiles(start, n, scores, update, ms + tail)

    rise = run(one_pass, (jnp.full((1, TQ), NEG, f32),))[N_HEADS]

    @pl.when(jnp.logical_not(jnp.max(rise) <= STALE_SHIFT_LIMIT))
    def _():
        run(exact, ())

    _write_output(acc_ref, o_ref)


def _fox_kernel(q_ref, k_ref, e_ref, vt_ref, o_ref, acc_ref):
    t = pl.program_id(1)
    keep = _diag_keep()
    row = lax.broadcasted_iota(jnp.int32, (LANE, TQ), 0)
    q_aug = []
    for h, qt in enumerate(_queries_t(q_ref)):
        first = (h % 2) * N_PIECES
        ones = jnp.where((row >= first) & (row < first + N_PIECES), 1.0, 0.0).astype(bf16)
        q_aug.append(jnp.concatenate([qt, ones], axis=0))

    def scores(j):
        off = pl.multiple_of(j * TK, TK)
        out = []
        for h in range(N_HEADS):
            if h % 2 == 0:
                k_aug = jnp.concatenate([k_ref[pl.ds(off, TK), _slab(h)], e_ref[pl.ds(off, TK), _slab(h)]], axis=1)
            out.append(jnp.dot(k_aug, q_aug[h], preferred_element_type=f32))
        return out

    def masked(j, h, s, diagonal):
        return (jnp.where(keep, s, NEG) if diagonal else s), None

    _attend(t, 0, t, scores, masked, vt_ref, acc_ref, o_ref)


def _fox(p3d, extras, vt):
    b, s, _ = p3d.shape
    return pl.pallas_call(
        _fox_kernel,
        out_shape=jax.ShapeDtypeStruct((b, s, WIDTH), bf16),
        grid=(b, s // TQ),
        in_specs=[
            pl.BlockSpec((None, TQ, WIDTH), lambda i, t: (i, t, SEC_QC)),
            pl.BlockSpec((None, s, WIDTH), lambda i, t: (i, 0, SEC_KC)),
            pl.BlockSpec((None, s, WIDTH), lambda i, t: (i, 0, 0)),
            pl.BlockSpec((WIDTH, s), lambda i, t: (2, i)),
        ],
        out_specs=pl.BlockSpec((None, TQ, WIDTH), lambda i, t: (i, t, 0)),
        scratch_shapes=[pltpu.VMEM((N_HEADS, ACC_ROWS, TQ), f32)],
        compiler_params=pltpu.CompilerParams(
            dimension_semantics=("arbitrary", "arbitrary"), vmem_limit_bytes=VMEM_LIMIT_BYTES),
        name="fox",
    )(p3d, p3d, extras, vt)


def _moba_kernel(q_ref, k_ref, vt_ref, o_ref, acc_ref, km_ref, sel_ref, *, n_blk):
    t = pl.program_id(1)
    nb_pad = km_ref.shape[0]

    @pl.when(t == 0)
    def _():
        km_ref[...] = jnp.zeros_like(km_ref)
        for blk in range(n_blk):
            kb = k_ref[blk * MOBA_BLOCK:(blk + 1) * MOBA_BLOCK, :].astype(f32)
            km_ref[blk:blk + 1, :] = jnp.sum(kb, axis=0, keepdims=True) * (1.0 / MOBA_BLOCK)

    keep = _diag_keep()
    blk_row =lax.broadcasted_iota(jnp.int32, (nb_pad, TQ), 0)
    blk_row_f = blk_row.astype(f32)
    qt = _queries_t(q_ref)

    for h in range(N_HEADS):
        gate = jnp.dot(km_ref[:, _slab(h)].astype(bf16), qt[h], preferred_element_type=f32)
        g = jnp.where(blk_row < t, gate, NEG)
        sel = jnp.zeros((nb_pad, TQ), f32)
        for _ in range(min(MOBA_TOPK, n_blk - 1)):
            mx = jnp.max(g, axis=0, keepdims=True)
            first = jnp.min(jnp.where(g == mx, blk_row_f, float(nb_pad)), axis=0, keepdims=True)
            hit = blk_row_f == first
            sel = jnp.where(hit, jnp.where(mx > 0.5 * NEG, 1.0, sel), sel)
            g = jnp.where(hit, NEG, g)
        sel_ref[h] = sel

    def scores(j):
        off = pl.multiple_of(j * TK, TK)
        return [jnp.dot(k_ref[pl.ds(off, TK), _slab(h)], qt[h], preferred_element_type=f32)
                for h in range(N_HEADS)]

    def masked(j, h, s, diagonal):
        if diagonal:
            return jnp.where(keep, s, NEG), None
        return s, sel_ref[h, pl.ds(j, 1), :] > 0.5

    _attend(t, 0, t, scores, masked, vt_ref, acc_ref, o_ref)


def _moba(p3d, vt):
    b, s, _ = p3d.shape
    assert TQ == MOBA_BLOCK and TK == MOBA_BLOCK and s % MOBA_BLOCK == 0
    n_blk = s // MOBA_BLOCK
    nb_pad = -(-n_blk // BF16_ROWS) * BF16_ROWS
    return pl.pallas_call(
        functools.partial(_moba_kernel, n_blk=n_blk),
        out_shape=jax.ShapeDtypeStruct((b, s, WIDTH), bf16),
        grid=(b, n_blk),
        in_specs=[
            pl.BlockSpec((None, TQ, WIDTH), lambda i, t: (i, t, SEC_QA)),
            pl.BlockSpec((None, s, WIDTH), lambda i, t: (i, 0, SEC_KA)),
            pl.BlockSpec((WIDTH, s), lambda i, t: (0, i)),
        ],
        out_specs=pl.BlockSpec((None, TQ, WIDTH), lambda i, t: (i, t, 0)),
        scratch_shapes=[pltpu.VMEM((N_HEADS, ACC_ROWS, TQ), f32),
                        pltpu.VMEM((nb_pad, WIDTH), f32),
                        pltpu.VMEM((N_HEADS, nb_pad, TQ), f32)],
        compiler_params=pltpu.CompilerParams(
            dimension_semantics=("arbitrary", "arbitrary"), vmem_limit_bytes=VMEM_LIMIT_BYTES),
        name="moba",
    )(p3d, p3d, vt)


def _dsa_kernel(q_ref, k_ref, vt_ref, qi_ref, kk_ref, sm_ref, o_ref, acc_ref, sc_ref, *, top, seq):
    t = pl.program_id(1)
    n_kv = t + 1
    n_chunks = lax.shift_right_logical(n_kv + 1, 1)
    k_top = float(top)
    keep = _diag_keep()
    chunk_row = lax.broadcasted_iota(jnp.int32, (2 * TK, TQ), 0)
    group_row = lax.broadcasted_iota(jnp.int32, (SUBLANE, TQ), 0)
    w_rows = sm_ref[...].T
    qit = _queries_t(qi_ref)

    @pl.when((n_kv & 1) == 1)
    def _():
        sc_ref[pl.ds(pl.multiple_of(n_kv * TK, TK), TK), :] = jnp.full((TK, TQ), -jnp.inf, f32)

    def score_body(j, _):
        off = pl.multiple_of(j * TK, TK)
        kkb = kk_ref[pl.ds(off, TK), :]
        rel = [jnp.dot(kkb, qit[hh], preferred_element_type=f32) for hh in range(IDX_HEADS)]
        isc = jnp.zeros((TK, TQ), f32)
        for hh in range(IDX_HEADS):
            isc = isc + w_rows[SMALL_WIDX + hh:SMALL_WIDX + hh + 1, :] * jnp.maximum(rel[hh], 0.0)
        sc_ref[pl.ds(off, TK), :] = jnp.where((j < t) | keep, isc, -jnp.inf)
        return 0

    lax.fori_loop(0, n_kv, score_body, 0)

    def count(pred):
        def body(c, accs):
            off = pl.multiple_of(c * (2 * TK), 2 * TK)
            chunk = sc_ref[pl.ds(off, 2 * TK), :]
            accs = list(accs)
            for g in range(2 * TK // SUBLANE):
                rows = chunk[g * SUBLANE:(g + 1) * SUBLANE, :]
                hit = jnp.where(pred(rows, off + g * SUBLANE), 1.0, 0.0)
                accs[g % COUNT_CHAINS] = accs[g % COUNT_CHAINS] + hit
            return tuple(accs)
        accs = lax.fori_loop(0, n_chunks, body, tuple(jnp.zeros((SUBLANE, TQ), f32) for _ in range(COUNT_CHAINS)))
        return jnp.sum(sum(accs), axis=0, keepdims=True)

    def as_score(code):
        return pltpu.bitcast(jnp.where(code < 0, code ^ 0x7FFFFFFF, code), f32)

    c_pos = count(lambda sc, off: sc >= 0.0)
    code = jnp.where(c_pos >= k_top, 0, INT_MIN).astype(jnp.int32)
    n_ge = jnp.where(c_pos >= k_top, c_pos, (n_kv * TK).astype(f32))

    def bit_body(i, carry):
        code, n_ge = carry
        cand = code + jnp.left_shift(jnp.int32(1), 30 - i)
        cand_f = as_score(cand)
        c = count(lambda sc, off: sc >= cand_f)
        ok = (c >= k_top) & (cand_f > -jnp.inf)
        return jnp.where(ok, cand, code), jnp.where(ok, c, n_ge)

    code, n_ge = lax.fori_loop(0, 31, bit_body, (code, n_ge))
    has_thr = code > INT_MIN
    thr = jnp.where(has_thr, as_score(code), -jnp.inf)

    no_cut = jnp.where(has_thr, seq + 1, 0).astype(jnp.int32)
    over = jnp.max(jnp.where(has_thr & (n_ge > k_top), 1.0, 0.0))

    def tie_cut():
        need = k_top - count(lambda sc, off: sc > thr)
        n_bits = seq.bit_length()

        def cut_body(i, cut):
            cand = cut + jnp.left_shift(jnp.int32(1), (n_bits - 1) - i)
            c = count(lambda sc, first: (sc == thr) & (group_row + first < cand))
            return jnp.where(c <= need, cand, cut)

        cut = lax.fori_loop(0, n_bits, cut_body, jnp.zeros((1, TQ), jnp.int32))
        return jnp.where(has_thr, cut, 0)

    cut = lax.cond(over > 0.5, tie_cut, lambda: no_cut)

    def ceiling_body(c, _):
        off = pl.multiple_of(c * (2 * TK), 2 * TK)
        sc = sc_ref[pl.ds(off, 2 * TK), :]
        chosen = (sc > thr) | ((sc == thr) & (chunk_row + off < cut))
        sc_ref[pl.ds(off, 2 * TK), :] = jnp.where(chosen, -NEG, NEG).astype(f32)
        return 0

    lax.fori_loop(0, n_chunks, ceiling_body, 0)

    qt = _queries_t(q_ref)

    def scores(j):
        off = pl.multiple_of(j * TK, TK)
        return [jnp.dot(k_ref[pl.ds(off, TK), _slab(h)], qt[h], preferred_element_type=f32)
                for h in range(N_HEADS)]

    def masked(j, h, s, diagonal):
        ceiling = sc_ref[pl.ds(pl.multiple_of(j * TK, TK), TK), :]
        return jnp.minimum(s, ceiling), None

    _attend(0, 1, t, scores, masked, vt_ref, acc_ref, o_ref)


def _dsa(p3d, vt, kk, small3d):
    b, s, _ = p3d.shape
    top = min(DSA_TOPK_MAX, s // 4)
    return pl.pallas_call(
        functools.partial(_dsa_kernel, top=top, seq=s),
        out_shape=jax.ShapeDtypeStruct((b, s, WIDTH), bf16),
        grid=(b, s // TQ),
        in_specs=[
            pl.BlockSpec((None, TQ, WIDTH), lambda i, t: (i, t, SEC_QB)),
            pl.BlockSpec((None, s, WIDTH), lambda i, t: (i, 0, SEC_KB)),
            pl.BlockSpec((WIDTH, s), lambda i, t: (1, i)),
            pl.BlockSpec((None, TQ, WIDTH), lambda i, t: (i, t, SEC_QIDX)),
            pl.BlockSpec((None, s, LANE), lambda i, t: (i, 0, 0)),
            pl.BlockSpec((None, TQ, LANE), lambda i, t: (i, t, 0)),
        ],
        out_specs=pl.BlockSpec((None, TQ, WIDTH), lambda i, t: (i, t, 0)),
        scratch_shapes=[pltpu.VMEM((N_HEADS, ACC_ROWS, TQ), f32), pltpu.VMEM((s, TQ), f32)],
        compiler_params=pltpu.CompilerParams(
            dimension_semantics=("arbitrary", "arbitrary"), vmem_limit_bytes=VMEM_LIMIT_BYTES),
        name="dsa",
    )(p3d, p3d, vt, p3d, kk, small3d)


def _out_kernel(oa_ref, ob_ref, oc_ref, ga_ref, gb_ref, gc_ref, m0_ref, m1_ref, m2_ref,
                x_ref, wb_ref, wo_ref, fg_ref, y_ref, *, final_norm):
    merged = None
    for n, (o_ref, g_ref, mg_ref) in enumerate(((oa_ref, ga_ref, m0_ref), (ob_ref, gb_ref, m1_ref),
                                                (oc_ref, gc_ref, m2_ref))):
        gated = (o_ref[...].astype(f32) * g_ref[...].astype(f32)).astype(bf16)
        y = jnp.dot(gated, wb_ref[n], preferred_element_type=f32) * mg_ref[...].astype(f32)
        merged = y if merged is None else merged + y
    out = x_ref[...] + jnp.dot(merged.astype(bf16), wo_ref[...], preferred_element_type=f32)
    if final_norm:
        ms = jnp.mean(out * out, axis=-1, keepdims=True)
        out = out * lax.rsqrt(ms + RMS_EPS) * fg_ref[...]
    y_ref[...] = out


def _output(oa, ob, oc, p2d, x2d, w_branch, w_out, final_gain, final_norm):
    n_rows, d_model = x2d.shape
    tm = min(OUT_TM, n_rows)
    assert SEC_MERGE * WIDTH % d_model == 0
    mg0 = SEC_MERGE * WIDTH // d_model
    row = lambda c: (lambda i: (i, c))
    return pl.pallas_call(
        functools.partial(_out_kernel, final_norm=final_norm),
        out_shape=jax.ShapeDtypeStruct((n_rows, d_model), f32),
        grid=(n_rows // tm,),
        in_specs=[
            pl.BlockSpec((tm, WIDTH), row(0)), pl.BlockSpec((tm, WIDTH), row(0)), pl.BlockSpec((tm, WIDTH), row(0)),
            pl.BlockSpec((tm, WIDTH), row(SEC_GA)), pl.BlockSpec((tm, WIDTH), row(SEC_GB)),
            pl.BlockSpec((tm, WIDTH), row(SEC_GC)),
            pl.BlockSpec((tm, d_model), row(mg0)), pl.BlockSpec((tm, d_model), row(mg0 + 1)),
            pl.BlockSpec((tm, d_model), row(mg0 + 2)),
            pl.BlockSpec((tm, d_model), row(0)),
            pl.BlockSpec((N_BRANCH, WIDTH, d_model), lambda i: (0, 0, 0)),
            pl.BlockSpec((d_model, d_model), lambda i: (0, 0)),
            pl.BlockSpec((1, d_model), lambda i: (0, 0)),
        ],
        out_specs=pl.BlockSpec((tm, d_model), row(0)),
        compiler_params=pltpu.CompilerParams(
            dimension_semantics=("arbitrary",), vmem_limit_bytes=VMEM_LIMIT_BYTES),
        name="out",
    )(oa, ob, oc, p2d, p2d, p2d, p2d, p2d, p2d, x2d, w_branch, w_out, final_gain)


def _pack_w_in(w):
    sec = lambda o: w[:, o * WIDTH:(o + 1) * WIDTH]
    small0 = 12 * WIDTH + IDX_HEADS * IDX_DIM
    merge0 = small0 + IDX_DIM + IDX_HEADS + N_HEADS
    order = (0, 1, 4, 5, 12, 8, 9, 3, 7, 11)
    main = jnp.concatenate([sec(o) for o in order] + [w[:, merge0:]], axis=1)
    small = jnp.pad(w[:, small0:merge0], ((0, 0), (0, LANE - (merge0 - small0))))
    values_t = jnp.concatenate([sec(2), sec(6), sec(10)], axis=1).T
    return main.astype(bf16), small.astype(bf16), values_t.astype(bf16)


def _rope_tables(seq):
    half = ROPE_DIM // 2
    inv_freq = jnp.power(jnp.float32(ROPE_THETA), -jnp.arange(0, ROPE_DIM, 2, dtype=f32) / ROPE_DIM)
    ang = jnp.arange(seq).astype(f32)[:, None] * inv_freq[None, :]
    cos, sin = jnp.cos(ang), jnp.sin(ang)
    pad = jnp.zeros((seq, HEAD_DIM - ROPE_DIM), f32)
    zero = jnp.zeros((seq, half), f32)
    tile = lambda a: jnp.tile(a, (1, LANE // HEAD_DIM))
    cos_t = tile(jnp.concatenate([cos, cos, pad + 1.0], axis=1))
    sin_a = tile(jnp.concatenate([-sin, zero, pad], axis=1))
    sin_b = tile(jnp.concatenate([zero, sin, pad], axis=1))
    return cos_t, sin_a, sin_b


def kernel(x, norm_gain, w_in, forget_bias, w_branch, w_out, final_gain):
    b, s, d_model = x.shape
    depth = norm_gain.shape[0]
    assert s % TQ == 0 and s % min(PROJ_TM, s) == 0 and s % min(CSUM_CHUNK, s) == 0
    cos_t, sin_a, sin_b = _rope_tables(s)
    h2d = x.reshape(b * s, d_model)
    for layer in range(depth):
        w_main, w_small, w_vt = _pack_w_in(w_in[layer])
        gain = norm_gain[layer][None, :]
        fbias = jnp.zeros((1, LANE), f32).at[0, SMALL_F:SMALL_F + N_HEADS].set(forget_bias[layer])
        p2d, small2d = _project(h2d, gain, w_main, w_small, cos_t, sin_a, sin_b, fbias, s)
        vt = _project_values(h2d, gain, w_vt, s)
        p3d = p2d.reshape(b, s, N_SEC * WIDTH)
        small3d = small2d.reshape(b, s, LANE)
        extras = _cumsum_extras(small3d)
        k_idx = small3d[:, :, SMALL_KIDX:SMALL_KIDX + IDX_DIM].astype(bf16)
        kk = jnp.concatenate([k_idx, k_idx], axis=-1)
        oa = _moba(p3d, vt)
        ob = _dsa(p3d, vt, kk, small3d)
        oc = _fox(p3d, extras, vt)
        h2d = _output(oa.reshape(b * s, WIDTH), ob.reshape(b * s, WIDTH), oc.reshape(b * s, WIDTH),
                      p2d, h2d, w_branch[layer].astype(bf16), w_out[layer].astype(bf16),
                      final_gain[None, :], layer == depth - 1)
    return h2d.reshape(b, s, d_model)
```

```python
import functools

import jax
import jax.numpy as jnp
from jax import lax
from jax.experimental import pallas as pl
from jax.experimental.pallas import tpu as pltpu

HEAD_DIM = 64
N_HEADS = 8
WIDTH = N_HEADS * HEAD_DIM
N_BRANCH = 3
ROPE_DIM = HEAD_DIM // 4
ROPE_THETA = 500000.0
MOBA_BLOCK = 256
MOBA_TOPK = 3
IDX_HEADS = 8
IDX_DIM = 64
DSA_TOPK_MAX = 256
RMS_EPS = 1e-6
LOG2E = 1.4426950408889634
ATTN_SCALE = HEAD_DIM ** -0.5 * LOG2E
IDX_SCALE = IDX_HEADS ** -0.5 * IDX_DIM ** -0.5

LANE = 128
SUBLANE = 8
BF16_ROWS = 16
VMEM_LIMIT_BYTES = 48 * 1024 * 1024

TQ = 256
TK = 256
PROJ_TM = 2048
PROJ_SUB = 512
OUT_TM = 512
CSUM_CHUNK = 512
ACC_ROWS = HEAD_DIM + BF16_ROWS

NEG = -1e30
STALE_SHIFT_LIMIT = 64.0
INT_MIN = -(2 ** 31)
I16_MIN = -(2 ** 15)
SEARCH_CHUNK = 2 * TK
COUNT_CHAINS = 8

SEC_QA, SEC_KA, SEC_QB, SEC_KB, SEC_QIDX = 0, 1, 2, 3, 4
SEC_QC, SEC_KC = 5, 6
SEC_GA, SEC_GB, SEC_GC = 7, 8, 9
SEC_MERGE = 10
N_SEC = 16
SMALL_KIDX, SMALL_WIDX, SMALL_F = 0, 64, 72
N_PIECES = 3

f32 = jnp.float32
bf16 = jnp.bfloat16


def _nt_dot(a, b):
    return lax.dot_general(a, b, (((1,), (1,)), ((), ())), preferred_element_type=f32)


def _split3(x):
    hi = x.astype(bf16)
    r1 = x - hi.astype(f32)
    mid = r1.astype(bf16)
    lo = (r1 - mid.astype(f32)).astype(bf16)
    return hi, mid, lo


def _rope(x, cos, sin_a, sin_b):
    return x * cos + pltpu.roll(x, LANE - ROPE_DIM // 2, 1) * sin_a + pltpu.roll(x, ROPE_DIM // 2, 1) * sin_b


def _rms_norm_bf16(x_ref, gain_ref):
    xf = x_ref[...]
    ms = jnp.mean(xf * xf, axis=-1, keepdims=True)
    return (xf * lax.rsqrt(ms + RMS_EPS) * gain_ref[...]).astype(bf16)


def _proj_kernel(x_ref, gain_ref, w_ref, ws_ref, cos_ref, sa_ref, sb_ref, fb_ref,
                 p_ref, small_ref, h_ref):
    j = pl.program_id(1)

    @pl.when(j == 0)
    def _():
        h_ref[...] = _rms_norm_bf16(x_ref, gain_ref)
        sm = jnp.dot(h_ref[...], ws_ref[...], preferred_element_type=f32)
        lane = lax.broadcasted_iota(jnp.int32, sm.shape, 1)
        z = sm + fb_ref[...]
        log_f = jnp.minimum(z, 0.0) - jnp.log1p(jnp.exp(-jnp.abs(z)))
        small_ref[...] = jnp.where(lane < SMALL_WIDX, _rope(sm, cos_ref[...], sa_ref[...], sb_ref[...]),
                                   jnp.where(lane < SMALL_F, sm * IDX_SCALE, log_f))

    def row_blocks(epilogue):
        for r in range(h_ref.shape[0] // PROJ_SUB):
            rows = slice(r * PROJ_SUB, (r + 1) * PROJ_SUB)
            acc = jnp.dot(h_ref[rows, :], w_ref[...], preferred_element_type=f32)
            p_ref[rows, :] = epilogue(acc, rows).astype(bf16)

    def rope_all(acc, rows):
        return jnp.concatenate(
            [_rope(acc[:, c * LANE:(c + 1) * LANE], cos_ref[rows, :], sa_ref[rows, :], sb_ref[rows, :])
             for c in range(WIDTH // LANE)], axis=1)

    @pl.when(j <= SEC_QIDX)
    def _():
        scale = jnp.where((j == SEC_QA) | (j == SEC_QB), ATTN_SCALE, 1.0).astype(f32)
        row_blocks(lambda acc, rows: rope_all(acc, rows) * scale)

    @pl.when((j > SEC_QIDX) & (j < SEC_GA))
    def _():
        scale = jnp.where(j == SEC_QC, ATTN_SCALE, 1.0).astype(f32)
        row_blocks(lambda acc, rows: acc * scale)

    @pl.when((j >= SEC_GA) & (j < SEC_MERGE))
    def _():
        row_blocks(lambda acc, rows: acc * jax.nn.sigmoid(acc))

    @pl.when(j >= SEC_MERGE)
    def _():
        row_blocks(lambda acc, rows: jax.nn.sigmoid(acc))


def _project(x2d, gain, w_main, w_small, cos, sin_a, sin_b, fbias, seq):
    n_rows, d_model = x2d.shape
    tm = min(PROJ_TM, seq)
    n_seq_tiles = seq // tm
    return pl.pallas_call(
        _proj_kernel,
        out_shape=(jax.ShapeDtypeStruct((n_rows, N_SEC * WIDTH), bf16),
                   jax.ShapeDtypeStruct((n_rows, LANE), f32)),
        grid=(n_rows // tm, N_SEC),
        in_specs=[
            pl.BlockSpec((tm, d_model), lambda i, j: (i, 0)),
            pl.BlockSpec((1, d_model), lambda i, j: (0, 0)),
            pl.BlockSpec((d_model, WIDTH), lambda i, j: (0, j)),
            pl.BlockSpec((d_model, LANE), lambda i, j: (0, 0)),
            pl.BlockSpec((tm, LANE), lambda i, j: (i % n_seq_tiles, 0)),
            pl.BlockSpec((tm, LANE), lambda i, j: (i % n_seq_tiles, 0)),
            pl.BlockSpec((tm, LANE), lambda i, j: (i % n_seq_tiles, 0)),
            pl.BlockSpec((1, LANE), lambda i, j: (0, 0)),
        ],
        out_specs=(pl.BlockSpec((tm, WIDTH), lambda i, j: (i, j)),
                   pl.BlockSpec((tm, LANE), lambda i, j: (i, 0))),
        scratch_shapes=[pltpu.VMEM((tm, d_model), bf16)],
        compiler_params=pltpu.CompilerParams(
            dimension_semantics=("arbitrary", "arbitrary"), vmem_limit_bytes=VMEM_LIMIT_BYTES),
        name="proj",
    )(x2d, gain, w_main, w_small, cos, sin_a, sin_b, fbias)


def _vproj_kernel(x_ref, gain_ref, wt_ref, vt_ref, h_ref):
    @pl.when(pl.program_id(1) == 0)
    def _():
        h_ref[...] = _rms_norm_bf16(x_ref, gain_ref)

    vt_ref[...] = _nt_dot(wt_ref[...], h_ref[...]).astype(bf16)


def _project_values(x2d, gain, w_vt, seq):
    n_rows, d_model = x2d.shape
    tm = min(PROJ_TM, seq)
    return pl.pallas_call(
        _vproj_kernel,
        out_shape=jax.ShapeDtypeStruct((N_BRANCH * WIDTH, n_rows), bf16),
        grid=(n_rows // tm, N_BRANCH),
        in_specs=[
            pl.BlockSpec((tm, d_model), lambda i, j: (i, 0)),
            pl.BlockSpec((1, d_model), lambda i, j: (0, 0)),
            pl.BlockSpec((WIDTH, d_model), lambda i, j: (j, 0)),
        ],
        out_specs=pl.BlockSpec((WIDTH, tm), lambda i, j: (j, i)),
        scratch_shapes=[pltpu.VMEM((tm, d_model), bf16)],
        compiler_params=pltpu.CompilerParams(
            dimension_semantics=("arbitrary", "arbitrary"), vmem_limit_bytes=VMEM_LIMIT_BYTES),
        name="vproj",
    )(x2d, gain, w_vt)


def _cumsum_kernel(x_ref, e_ref, carry_ref):
    @pl.when(pl.program_id(1) == 0)
    def _():
        carry_ref[...] = jnp.zeros_like(carry_ref)

    x = x_ref[...]
    ch = x.shape[0]
    tri = (lax.broadcasted_iota(jnp.int32, (ch, ch), 1)
           <= lax.broadcasted_iota(jnp.int32, (ch, ch), 0)).astype(bf16)
    cs = sum(jnp.dot(tri, piece, preferred_element_type=f32) for piece in _split3(x)) + carry_ref[0:1, :]
    carry_ref[...] = jnp.broadcast_to(cs[ch - 1:ch, :], carry_ref.shape)

    src = lax.broadcasted_iota(jnp.int32, (LANE, WIDTH), 0) - SMALL_F
    dst = lax.broadcasted_iota(jnp.int32, (LANE, WIDTH), 1)
    is_head = (src >= 0) & (src < N_HEADS)
    base = (src >> 1) * LANE + (src & 1) * N_PIECES
    out = None
    for n, piece in enumerate(_split3(cs * -LOG2E)):
        place = jnp.where(is_head & (dst == base + n), 1.0, 0.0).astype(bf16)
        term = jnp.dot(piece, place, preferred_element_type=f32)
        out = term if out is None else out + term
    e_ref[...] = out.astype(bf16)


def _cumsum_extras(small3d):
    b, s, _ = small3d.shape
    ch = min(CSUM_CHUNK, s)
    return pl.pallas_call(
        _cumsum_kernel,
        out_shape=jax.ShapeDtypeStruct((b, s, WIDTH), bf16),
        grid=(b, s // ch),
        in_specs=[pl.BlockSpec((None, ch, LANE), lambda i, c: (i, c, 0))],
        out_specs=pl.BlockSpec((None, ch, WIDTH), lambda i, c: (i, c, 0)),
        scratch_shapes=[pltpu.VMEM((SUBLANE, LANE), f32)],
        compiler_params=pltpu.CompilerParams(
            dimension_semantics=("arbitrary", "arbitrary"), vmem_limit_bytes=VMEM_LIMIT_BYTES),
        name="csum",
    )(small3d)


def _slab(h):
    return slice((h // 2) * LANE, (h // 2 + 1) * LANE)


def _queries_t(q_ref):
    row = lax.broadcasted_iota(jnp.int32, (LANE, TQ), 0)
    out = []
    for h in range(N_HEADS):
        if h % 2 == 0:
            qt = q_ref[:, _slab(h)].astype(f32).T
        first = HEAD_DIM * (h % 2)
        out.append(jnp.where((row >= first) & (row < first + HEAD_DIM), qt, 0.0).astype(bf16))
    return out


def _values_with_ones(vt_ref, h, off):
    return jnp.concatenate([vt_ref[h * HEAD_DIM:(h + 1) * HEAD_DIM, pl.ds(off, TK)],
                            jnp.ones((BF16_ROWS, TK), bf16)], axis=0)


def _online_update(s, m, vt, acc_ref, h, picked=None):
    mx = jnp.max(s, axis=0, keepdims=True)
    if picked is not None:
        mx = jnp.where(picked, mx, NEG)
    m_new = jnp.maximum(m, mx)
    shift = m_new if picked is None else jnp.where(picked, m_new, -NEG)
    p = jnp.exp2(s - shift).astype(bf16)
    acc_ref[h] = jnp.exp2(m - m_new) * acc_ref[h] + jnp.dot(vt, p, preferred_element_type=f32)
    return m_new


def _stale_shift_update(s, c, vt, acc_ref, h, picked=None):
    mx = jnp.max(s, axis=0, keepdims=True)
    if picked is not None:
        mx = jnp.where(picked, mx, NEG)
    shift = c if picked is None else jnp.where(picked, c, -NEG)
    p = jnp.exp2(s - shift).astype(bf16)
    c_new = jnp.maximum(c, mx)
    acc_ref[h] = jnp.exp2(c - c_new) * (acc_ref[h] + jnp.dot(vt, p, preferred_element_type=f32))
    return c_new, mx - c


def _write_output(acc_ref, o_ref):
    for pair in range(N_HEADS // 2):
        halves = []
        for h in (2 * pair, 2 * pair + 1):
            a = acc_ref[h]
            halves.append(a[:HEAD_DIM] / a[HEAD_DIM:HEAD_DIM + 1])
        o_ref[:, pair * LANE:(pair + 1) * LANE] = jnp.concatenate(halves, axis=0).T.astype(bf16)


def _diag_keep():
    return lax.broadcasted_iota(jnp.int32, (TK, TQ), 0) <= lax.broadcasted_iota(jnp.int32, (TK, TQ), 1)


def _init_rows():
    return tuple(jnp.full((1, TQ), NEG, f32) for _ in range(N_HEADS))


def _for_key_tiles(start, n, scores, update, carry):
    def pair(i, c):
        j = start + 2 * i
        s_a, s_b = scores(j), scores(j + 1)
        return update(j + 1, s_b, update(j, s_a, c))

    carry = lax.fori_loop(0, lax.shift_right_logical(n, 1), pair, carry)
    last = start + n - 1
    return lax.cond((n & 1) == 1, lambda c: update(last, scores(last), c), lambda c: c, carry)


def _make_updates(masked, vt_ref, acc_ref):
    def exact(j, sc, ms, diagonal=False):
        off = pl.multiple_of(j * TK, TK)
        out = []
        for h, s in enumerate(sc):
            s, picked = masked(j, h, s, diagonal)
            out.append(_online_update(s, ms[h], _values_with_ones(vt_ref, h, off), acc_ref, h, picked))
        return tuple(out)

    def one_pass(j, sc, carry):
        off = pl.multiple_of(j * TK, TK)
        rise, out = carry[N_HEADS], []
        for h, s in enumerate(sc):
            s, picked = masked(j, h, s, False)
            c_new, up = _stale_shift_update(s, carry[h], _values_with_ones(vt_ref, h, off), acc_ref, h, picked)
            out.append(c_new)
            rise = jnp.maximum(rise, up)
        return tuple(out) + (rise,)

    return exact, one_pass


def _attend(first, start, n, scores, masked, vt_ref, acc_ref, o_ref):
    exact, one_pass = _make_updates(masked, vt_ref, acc_ref)

    def run(update, tail):
        acc_ref[...] = jnp.zeros_like(acc_ref)
        ms = exact(first, scores(first), _init_rows(), diagonal=True)
        return _for_key_tiles(start, n, scores, update, ms + tail)

    rise = run(one_pass, (jnp.full((1, TQ), NEG, f32),))[N_HEADS]

    @pl.when(jnp.logical_not(jnp.max(rise) <= STALE_SHIFT_LIMIT))
    def _():
        run(exact, ())

    _write_output(acc_ref, o_ref)


def _fox_kernel(q_ref, k_ref, e_ref, vt_ref, o_ref, acc_ref):
    t = pl.program_id(1)
    keep = _diag_keep()
    row = lax.broadcasted_iota(jnp.int32, (LANE, TQ), 0)
    q_aug = []
    for h, qt in enumerate(_queries_t(q_ref)):
        first = (h % 2) * N_PIECES
        ones = jnp.where((row >= first) & (row < first + N_PIECES), 1.0, 0.0).astype(bf16)
        q_aug.append(jnp.concatenate([qt, ones], axis=0))

    def scores(j):
        off = pl.multiple_of(j * TK, TK)
        out = []
        for h in range(N_HEADS):
            if h % 2 == 0:
                k_aug = jnp.concatenate([k_ref[pl.ds(off, TK), _slab(h)], e_ref[pl.ds(off, TK), _slab(h)]], axis=1)
            out.append(jnp.dot(k_aug, q_aug[h], preferred_element_type=f32))
        return out

    def masked(j, h, s, diagonal):
        return (jnp.where(keep, s, NEG) if diagonal else s), None

    _attend(t, 0, t, scores, masked, vt_ref, acc_ref, o_ref)


def _fox(p3d, extras, vt):
    b, s, _ = p3d.shape
    return pl.pallas_call(
        _fox_kernel,
        out_shape=jax.ShapeDtypeStruct((b, s, WIDTH), bf16),
        grid=(b, s // TQ),
        in_specs=[
            pl.BlockSpec((None, TQ, WIDTH), lambda i, t: (i, t, SEC_QC)),
            pl.BlockSpec((None, s, WIDTH), lambda i, t: (i, 0, SEC_KC)),
            pl.BlockSpec((None, s, WIDTH), lambda i, t: (i, 0, 0)),
            pl.BlockSpec((WIDTH, s), lambda i, t: (2, i)),
        ],
        out_specs=pl.BlockSpec((None, TQ, WIDTH), lambda i, t: (i, t, 0)),
        scratch_shapes=[pltpu.VMEM((N_HEADS, ACC_ROWS, TQ), f32)],
        compiler_params=pltpu.CompilerParams(
            dimension_semantics=("arbitrary", "arbitrary"), vmem_limit_bytes=VMEM_LIMIT_BYTES),
        name="fox",
    )(p3d, p3d, extras, vt)


def _moba_kernel(q_ref, k_ref, vt_ref, o_ref, acc_ref, km_ref, sel_ref, *, n_blk):
    t = pl.program_id(1)
    nb_pad = km_ref.shape[0]

    @pl.when(t == 0)
    def _():
        km_ref[...] = jnp.zeros_like(km_ref)
        for blk in range(n_blk):
            kb = k_ref[blk * MOBA_BLOCK:(blk + 1) * MOBA_BLOCK, :].astype(f32)
            km_ref[blk:blk + 1, :] = jnp.sum(kb, axis=0, keepdims=True) * (1.0 / MOBA_BLOCK)

    keep = _diag_keep()
    blk_row =lax.broadcasted_iota(jnp.int32, (nb_pad, TQ), 0)
    blk_row_f = blk_row.astype(f32)
    qt = _queries_t(q_ref)

    for h in range(N_HEADS):
        gate = jnp.dot(km_ref[:, _slab(h)].astype(bf16), qt[h], preferred_element_type=f32)
        g = jnp.where(blk_row < t, gate, NEG)
        sel = jnp.zeros((nb_pad, TQ), f32)
        for _ in range(min(MOBA_TOPK, n_blk - 1)):
            mx = jnp.max(g, axis=0, keepdims=True)
            first = jnp.min(jnp.where(g == mx, blk_row_f, float(nb_pad)), axis=0, keepdims=True)
            hit = blk_row_f == first
            sel = jnp.where(hit, jnp.where(mx > 0.5 * NEG, 1.0, sel), sel)
            g = jnp.where(hit, NEG, g)
        sel_ref[h] = sel

    def scores(j):
        off = pl.multiple_of(j * TK, TK)
        return [jnp.dot(k_ref[pl.ds(off, TK), _slab(h)], qt[h], preferred_element_type=f32)
                for h in range(N_HEADS)]

    def masked(j, h, s, diagonal):
        if diagonal:
            return jnp.where(keep, s, NEG), None
        return s, sel_ref[h, pl.ds(j, 1), :] > 0.5

    _attend(t, 0, t, scores, masked, vt_ref, acc_ref, o_ref)


def _moba(p3d, vt):
    b, s, _ = p3d.shape
    assert TQ == MOBA_BLOCK and TK == MOBA_BLOCK and s % MOBA_BLOCK == 0
    n_blk = s // MOBA_BLOCK
    nb_pad = -(-n_blk // BF16_ROWS) * BF16_ROWS
    return pl.pallas_call(
        functools.partial(_moba_kernel, n_blk=n_blk),
        out_shape=jax.ShapeDtypeStruct((b, s, WIDTH), bf16),
        grid=(b, n_blk),
        in_specs=[
            pl.BlockSpec((None, TQ, WIDTH), lambda i, t: (i, t, SEC_QA)),
            pl.BlockSpec((None, s, WIDTH), lambda i, t: (i, 0, SEC_KA)),
            pl.BlockSpec((WIDTH, s), lambda i, t: (0, i)),
        ],
        out_specs=pl.BlockSpec((None, TQ, WIDTH), lambda i, t: (i, t, 0)),
        scratch_shapes=[pltpu.VMEM((N_HEADS, ACC_ROWS, TQ), f32),
                        pltpu.VMEM((nb_pad, WIDTH), f32),
                        pltpu.VMEM((N_HEADS, nb_pad, TQ), f32)],
        compiler_params=pltpu.CompilerParams(
            dimension_semantics=("arbitrary", "arbitrary"), vmem_limit_bytes=VMEM_LIMIT_BYTES),
        name="moba",
    )(p3d, p3d, vt)


def _dsa_kernel(q_ref, k_ref, vt_ref, qi_ref, kk_ref, sm_ref, o_ref, acc_ref, sc_ref, sc16_ref, *, top, seq):
    t = pl.program_id(1)
    n_kv = t + 1
    n_chunks = lax.shift_right_logical(n_kv + 1, 1)
    n_keys = (n_kv * TK).astype(f32)
    k_top = float(top)
    keep = _diag_keep()
    chunk_row = lax.broadcasted_iota(jnp.int32, (SEARCH_CHUNK, TQ), 0)
    group_row = lax.broadcasted_iota(jnp.int32, (SUBLANE, TQ), 0)
    w_rows = sm_ref[...].T
    qit = _queries_t(qi_ref)

    @pl.when((n_kv & 1) == 1)
    def _():
        pad = pl.ds(pl.multiple_of(n_kv * TK, TK), TK)
        sc_ref[pad, :] = jnp.full((TK, TQ), -jnp.inf, f32)
        sc16_ref[pad, :] = jnp.full((TK, TQ), -jnp.inf, bf16)

    def score_body(j, _):
        off = pl.multiple_of(j * TK, TK)
        kkb = kk_ref[pl.ds(off, TK), :]
        rel = [jnp.dot(kkb, qit[hh], preferred_element_type=f32) for hh in range(IDX_HEADS)]
        isc = jnp.zeros((TK, TQ), f32)
        for hh in range(IDX_HEADS):
            isc = isc + w_rows[SMALL_WIDX + hh:SMALL_WIDX + hh + 1, :] * jnp.maximum(rel[hh], 0.0)
        isc = jnp.where((j < t) | keep, isc, -jnp.inf)
        sc_ref[pl.ds(off, TK), :] = isc
        sc16_ref[pl.ds(off, TK), :] = isc.astype(bf16)
        return 0

    lax.fori_loop(0, n_kv, score_body, 0)

    def count_in(ref, rows_per_vreg, one, pred):
        def body(c, accs):
            off = pl.multiple_of(c * SEARCH_CHUNK, SEARCH_CHUNK)
            chunk = ref[pl.ds(off, SEARCH_CHUNK), :]
            accs = list(accs)
            for g in range(SEARCH_CHUNK // rows_per_vreg):
                rows = chunk[g * rows_per_vreg:(g + 1) * rows_per_vreg, :]
                hit = jnp.where(pred(rows, off + g * rows_per_vreg), one, one * 0)
                accs[g % COUNT_CHAINS] = accs[g % COUNT_CHAINS] + hit
            return tuple(accs)
        init = tuple(jnp.zeros((rows_per_vreg, TQ), one.dtype) for _ in range(COUNT_CHAINS))
        accs = lax.fori_loop(0, n_chunks, body, init)
        return jnp.sum(sum(a.astype(f32) for a in accs), axis=0, keepdims=True)

    def count(pred):
        return count_in(sc_ref, SUBLANE, jnp.float32(1), pred)

    def count16(pred):
        return count_in(sc16_ref, BF16_ROWS, jnp.bfloat16(1), pred)

    def as_score(code):
        return pltpu.bitcast(jnp.where(code < 0, code ^ 0x7FFFFFFF, code), f32)

    def as_score16(code16):
        return pltpu.bitcast(jnp.left_shift(jnp.where(code16 < 0, code16 ^ 0x7FFF, code16), 16), f32)

    def search(n_bits, first, value_of, count_ge, floor_count):
        def bit_body(i, carry):
            code, n_ge = carry
            cand = code + jnp.left_shift(jnp.int32(1), (n_bits - 1) - i)
            cand_f = value_of(cand)
            c = count_ge(cand_f)
            ok = (c >= k_top) & (cand_f > -jnp.inf)
            return jnp.where(ok, cand, code), jnp.where(ok, c, n_ge)
        return lax.fori_loop(0, n_bits, bit_body, (first, floor_count))

    i16_min = jnp.full((1, TQ), I16_MIN, jnp.int32)
    code16, _ = search(16, i16_min, as_score16, lambda v: count16(lambda r, _: r >= v.astype(bf16)),
                       jnp.broadcast_to(n_keys, (1, TQ)))
    has_thr = code16 > I16_MIN
    centre = code16 * 65536 + jnp.where(code16 < 0, 65535, 0)
    first = jnp.where(has_thr, centre - 65536, INT_MIN)
    n_first = jnp.where(has_thr, count(lambda sc, _: sc >= as_score(first)), n_keys)
    code, n_ge = search(17, first, as_score, lambda v: count(lambda sc, _: sc >= v), n_first)
    thr = jnp.where(has_thr, as_score(code), -jnp.inf)

    no_cut = jnp.where(has_thr, seq + 1, 0).astype(jnp.int32)
    over = jnp.max(jnp.where(has_thr & (n_ge > k_top), 1.0, 0.0))

    def tie_cut():
        need = k_top - count(lambda sc, off: sc > thr)
        n_bits = seq.bit_length()

        def cut_body(i, cut):
            cand = cut + jnp.left_shift(jnp.int32(1), (n_bits - 1) - i)
            c = count(lambda sc, first: (sc == thr) & (group_row + first < cand))
            return jnp.where(c <= need, cand, cut)

        cut = lax.fori_loop(0, n_bits, cut_body, jnp.zeros((1, TQ), jnp.int32))
        return jnp.where(has_thr, cut, 0)

    cut = lax.cond(over > 0.5, tie_cut, lambda: no_cut)

    def ceiling_body(c, _):
        off = pl.multiple_of(c * SEARCH_CHUNK, SEARCH_CHUNK)
        sc = sc_ref[pl.ds(off, SEARCH_CHUNK), :]
        chosen = (sc > thr) | ((sc == thr) & (chunk_row + off < cut))
        sc_ref[pl.ds(off, SEARCH_CHUNK), :] = jnp.where(chosen, -NEG, NEG).astype(f32)
        return 0

    lax.fori_loop(0, n_chunks, ceiling_body, 0)

    qt = _queries_t(q_ref)

    def scores(j):
        off = pl.multiple_of(j * TK, TK)
        return [jnp.dot(k_ref[pl.ds(off, TK), _slab(h)], qt[h], preferred_element_type=f32)
                for h in range(N_HEADS)]

    def masked(j, h, s, diagonal):
        ceiling = sc_ref[pl.ds(pl.multiple_of(j * TK, TK), TK), :]
        return jnp.minimum(s, ceiling), None

    _attend(0, 1, t, scores, masked, vt_ref, acc_ref, o_ref)


def _dsa(p3d, vt, kk, small3d):
    b, s, _ = p3d.shape
    top = min(DSA_TOPK_MAX, s // 4)
    return pl.pallas_call(
        functools.partial(_dsa_kernel, top=top, seq=s),
        out_shape=jax.ShapeDtypeStruct((b, s, WIDTH), bf16),
        grid=(b, s // TQ),
        in_specs=[
            pl.BlockSpec((None, TQ, WIDTH), lambda i, t: (i, t, SEC_QB)),
            pl.BlockSpec((None, s, WIDTH), lambda i, t: (i, 0, SEC_KB)),
            pl.BlockSpec((WIDTH, s), lambda i, t: (1, i)),
            pl.BlockSpec((None, TQ, WIDTH), lambda i, t: (i, t, SEC_QIDX)),
            pl.BlockSpec((None, s, LANE), lambda i, t: (i, 0, 0)),
            pl.BlockSpec((None, TQ, LANE), lambda i, t: (i, t, 0)),
        ],
        out_specs=pl.BlockSpec((None, TQ, WIDTH), lambda i, t: (i, t, 0)),
        scratch_shapes=[pltpu.VMEM((N_HEADS, ACC_ROWS, TQ), f32), pltpu.VMEM((s, TQ), f32), pltpu.VMEM((s, TQ), bf16)],
        compiler_params=pltpu.CompilerParams(
            dimension_semantics=("arbitrary", "arbitrary"), vmem_limit_bytes=VMEM_LIMIT_BYTES),
        name="dsa",
    )(p3d, p3d, vt, p3d, kk, small3d)


def _out_kernel(oa_ref, ob_ref, oc_ref, ga_ref, gb_ref, gc_ref, m0_ref, m1_ref, m2_ref,
                x_ref, wb_ref, wo_ref, fg_ref, y_ref, *, final_norm):
    merged = None
    for n, (o_ref, g_ref, mg_ref) in enumerate(((oa_ref, ga_ref, m0_ref), (ob_ref, gb_ref, m1_ref),
                                                (oc_ref, gc_ref, m2_ref))):
        gated = (o_ref[...].astype(f32) * g_ref[...].astype(f32)).astype(bf16)
        y = jnp.dot(gated, wb_ref[n], preferred_element_type=f32) * mg_ref[...].astype(f32)
        merged = y if merged is None else merged + y
    out = x_ref[...] + jnp.dot(merged.astype(bf16), wo_ref[...], preferred_element_type=f32)
    if final_norm:
        ms = jnp.mean(out * out, axis=-1, keepdims=True)
        out = out * lax.rsqrt(ms + RMS_EPS) * fg_ref[...]
    y_ref[...] = out


def _output(oa, ob, oc, p2d, x2d, w_branch, w_out, final_gain, final_norm):
    n_rows, d_model = x2d.shape
    tm = min(OUT_TM, n_rows)
    assert SEC_MERGE * WIDTH % d_model == 0
    mg0 = SEC_MERGE * WIDTH // d_model
    row = lambda c: (lambda i: (i, c))
    return pl.pallas_call(
        functools.partial(_out_kernel, final_norm=final_norm),
        out_shape=jax.ShapeDtypeStruct((n_rows, d_model), f32),
        grid=(n_rows // tm,),
        in_specs=[
            pl.BlockSpec((tm, WIDTH), row(0)), pl.BlockSpec((tm, WIDTH), row(0)), pl.BlockSpec((tm, WIDTH), row(0)),
            pl.BlockSpec((tm, WIDTH), row(SEC_GA)), pl.BlockSpec((tm, WIDTH), row(SEC_GB)),
            pl.BlockSpec((tm, WIDTH), row(SEC_GC)),
            pl.BlockSpec((tm, d_model), row(mg0)), pl.BlockSpec((tm, d_model), row(mg0 + 1)),
            pl.BlockSpec((tm, d_model), row(mg0 + 2)),
            pl.BlockSpec((tm, d_model), row(0)),
            pl.BlockSpec((N_BRANCH, WIDTH, d_model), lambda i: (0, 0, 0)),
            pl.BlockSpec((d_model, d_model), lambda i: (0, 0)),
            pl.BlockSpec((1, d_model), lambda i: (0, 0)),
        ],
        out_specs=pl.BlockSpec((tm, d_model), row(0)),
        compiler_params=pltpu.CompilerParams(
            dimension_semantics=("arbitrary",), vmem_limit_bytes=VMEM_LIMIT_BYTES),
        name="out",
    )(oa, ob, oc, p2d, p2d, p2d, p2d, p2d, p2d, x2d, w_branch, w_out, final_gain)


def _pack_w_in(w):
    sec = lambda o: w[:, o * WIDTH:(o + 1) * WIDTH]
    small0 = 12 * WIDTH + IDX_HEADS * IDX_DIM
    merge0 = small0 + IDX_DIM + IDX_HEADS + N_HEADS
    order = (0, 1, 4, 5, 12, 8, 9, 3, 7, 11)
    main = jnp.concatenate([sec(o) for o in order] + [w[:, merge0:]], axis=1)
    small = jnp.pad(w[:, small0:merge0], ((0, 0), (0, LANE - (merge0 - small0))))
    values_t = jnp.concatenate([sec(2), sec(6), sec(10)], axis=1).T
    return main.astype(bf16), small.astype(bf16), values_t.astype(bf16)


def _rope_tables(seq):
    half = ROPE_DIM // 2
    inv_freq = jnp.power(jnp.float32(ROPE_THETA), -jnp.arange(0, ROPE_DIM, 2, dtype=f32) / ROPE_DIM)
    ang = jnp.arange(seq).astype(f32)[:, None] * inv_freq[None, :]
    cos, sin = jnp.cos(ang), jnp.sin(ang)
    pad = jnp.zeros((seq, HEAD_DIM - ROPE_DIM), f32)
    zero = jnp.zeros((seq, half), f32)
    tile = lambda a: jnp.tile(a, (1, LANE // HEAD_DIM))
    cos_t = tile(jnp.concatenate([cos, cos, pad + 1.0], axis=1))
    sin_a = tile(jnp.concatenate([-sin, zero, pad], axis=1))
    sin_b = tile(jnp.concatenate([zero, sin, pad], axis=1))
    return cos_t, sin_a, sin_b


def kernel(x, norm_gain, w_in, forget_bias, w_branch, w_out, final_gain):
    b, s, d_model = x.shape
    depth = norm_gain.shape[0]
    assert s % TQ == 0 and s % min(PROJ_TM, s) == 0 and s % min(CSUM_CHUNK, s) == 0
    cos_t, sin_a, sin_b = _rope_tables(s)
    h2d = x.reshape(b * s, d_model)
    for layer in range(depth):
        w_main, w_small, w_vt = _pack_w_in(w_in[layer])
        gain = norm_gain[layer][None, :]
        fbias = jnp.zeros((1, LANE), f32).at[0, SMALL_F:SMALL_F + N_HEADS].set(forget_bias[layer])
        p2d, small2d = _project(h2d, gain, w_main, w_small, cos_t, sin_a, sin_b, fbias, s)
        vt = _project_values(h2d, gain, w_vt, s)
        p3d = p2d.reshape(b, s, N_SEC * WIDTH)
        small3d = small2d.reshape(b, s, LANE)
        extras = _cumsum_extras(small3d)
        k_idx = small3d[:, :, SMALL_KIDX:SMALL_KIDX + IDX_DIM].astype(bf16)
        kk = jnp.concatenate([k_idx, k_idx], axis=-1)
        oa = _moba(p3d, vt)
        ob = _dsa(p3d, vt, kk, small3d)
        oc = _fox(p3d, extras, vt)
        h2d = _output(oa.reshape(b * s, WIDTH), ob.reshape(b * s, WIDTH), oc.reshape(b * s, WIDTH),
                      p2d, h2d, w_branch[layer].astype(bf16), w_out[layer].astype(bf16),
                      final_gain[None, :], layer == depth - 1)
    return h2d.reshape(b, s, d_model)
```

```python
import functools

import jax
import jax.numpy as jnp
from jax import lax
from jax.experimental import pallas as pl
from jax.experimental.pallas import tpu as pltpu

HEAD_DIM = 64
N_HEADS = 8
WIDTH = N_HEADS * HEAD_DIM
N_BRANCH = 3
ROPE_DIM = HEAD_DIM // 4
ROPE_THETA = 500000.0
MOBA_BLOCK = 256
MOBA_TOPK = 3
IDX_HEADS = 8
IDX_DIM = 64
DSA_TOPK_MAX = 256
RMS_EPS = 1e-6
LOG2E = 1.4426950408889634
ATTN_SCALE = HEAD_DIM ** -0.5 * LOG2E
IDX_SCALE = IDX_HEADS ** -0.5 * IDX_DIM ** -0.5

LANE = 128
SUBLANE = 8
BF16_ROWS = 16
VMEM_LIMIT_BYTES = 48 * 1024 * 1024

TQ = 256
TK = 256
PROJ_TM = 2048
PROJ_SUB = 256
OUT_TM = 1024
CSUM_CHUNK = 512
ACC_ROWS = HEAD_DIM + BF16_ROWS

NEG = -1e30
STALE_SHIFT_LIMIT = 64.0
INT_MIN = -(2 ** 31)
I16_MIN = -(2 ** 15)
SEARCH_CHUNK = 2 * TK
COUNT_CHAINS = 2

SEC_QA, SEC_KA, SEC_QB, SEC_KB, SEC_QIDX = 0, 1, 2, 3, 4
SEC_QC, SEC_KC = 5, 6
SEC_GA, SEC_GB, SEC_GC = 7, 8, 9
SEC_MERGE = 10
N_SEC = 16
SMALL_KIDX, SMALL_WIDX, SMALL_F = 0, 64, 72
N_PIECES = 3

f32 = jnp.float32
bf16 = jnp.bfloat16


def _nt_dot(a, b):
    return lax.dot_general(a, b, (((1,), (1,)), ((), ())), preferred_element_type=f32)


def _split3(x):
    hi = x.astype(bf16)
    r1 = x - hi.astype(f32)
    mid = r1.astype(bf16)
    lo = (r1 - mid.astype(f32)).astype(bf16)
    return hi, mid, lo


def _rope(x, cos, sin_a, sin_b):
    return x * cos + pltpu.roll(x, LANE - ROPE_DIM // 2, 1) * sin_a + pltpu.roll(x, ROPE_DIM // 2, 1) * sin_b


def _rms_norm_bf16(x_ref, gain_ref):
    xf = x_ref[...]
    ms = jnp.mean(xf * xf, axis=-1, keepdims=True)
    return (xf * lax.rsqrt(ms + RMS_EPS) * gain_ref[...]).astype(bf16)


def _proj_kernel(x_ref, gain_ref, w_ref, ws_ref, cos_ref, sa_ref, sb_ref, fb_ref,
                 p_ref, small_ref, h_ref):
    j = pl.program_id(1)

    @pl.when(j == 0)
    def _():
        h_ref[...] = _rms_norm_bf16(x_ref, gain_ref)
        sm = jnp.dot(h_ref[...], ws_ref[...], preferred_element_type=f32)
        lane = lax.broadcasted_iota(jnp.int32, sm.shape, 1)
        z = sm + fb_ref[...]
        log_f = jnp.minimum(z, 0.0) - jnp.log1p(jnp.exp(-jnp.abs(z)))
        small_ref[...] = jnp.where(lane < SMALL_WIDX, _rope(sm, cos_ref[...], sa_ref[...], sb_ref[...]),
                                   jnp.where(lane < SMALL_F, sm * IDX_SCALE, log_f))

    def row_blocks(epilogue):
        for r in range(h_ref.shape[0] // PROJ_SUB):
            rows = slice(r * PROJ_SUB, (r + 1) * PROJ_SUB)
            acc = jnp.dot(h_ref[rows, :], w_ref[...], preferred_element_type=f32)
            p_ref[rows, :] = epilogue(acc, rows).astype(bf16)

    def rope_all(acc, rows):
        return jnp.concatenate(
            [_rope(acc[:, c * LANE:(c + 1) * LANE], cos_ref[rows, :], sa_ref[rows, :], sb_ref[rows, :])
             for c in range(WIDTH // LANE)], axis=1)

    @pl.when(j <= SEC_QIDX)
    def _():
        scale = jnp.where((j == SEC_QA) | (j == SEC_QB), ATTN_SCALE, 1.0).astype(f32)
        row_blocks(lambda acc, rows: rope_all(acc, rows) * scale)

    @pl.when((j > SEC_QIDX) & (j < SEC_GA))
    def _():
        scale = jnp.where(j == SEC_QC, ATTN_SCALE, 1.0).astype(f32)
        row_blocks(lambda acc, rows: acc * scale)

    @pl.when((j >= SEC_GA) & (j < SEC_MERGE))
    def _():
        row_blocks(lambda acc, rows: acc * jax.nn.sigmoid(acc))

    @pl.when(j >= SEC_MERGE)
    def _():
        row_blocks(lambda acc, rows: jax.nn.sigmoid(acc))


def _project(x2d, gain, w_main, w_small, cos, sin_a, sin_b, fbias, seq):
    n_rows, d_model = x2d.shape
    tm = min(PROJ_TM, seq)
    n_seq_tiles = seq // tm
    return pl.pallas_call(
        _proj_kernel,
        out_shape=(jax.ShapeDtypeStruct((n_rows, N_SEC * WIDTH), bf16),
                   jax.ShapeDtypeStruct((n_rows, LANE), f32)),
        grid=(n_rows // tm, N_SEC),
        in_specs=[
            pl.BlockSpec((tm, d_model), lambda i, j: (i, 0)),
            pl.BlockSpec((1, d_model), lambda i, j: (0, 0)),
            pl.BlockSpec((d_model, WIDTH), lambda i, j: (0, j)),
            pl.BlockSpec((d_model, LANE), lambda i, j: (0, 0)),
            pl.BlockSpec((tm, LANE), lambda i, j: (i % n_seq_tiles, 0)),
            pl.BlockSpec((tm, LANE), lambda i, j: (i % n_seq_tiles, 0)),
            pl.BlockSpec((tm, LANE), lambda i, j: (i % n_seq_tiles, 0)),
            pl.BlockSpec((1, LANE), lambda i, j: (0, 0)),
        ],
        out_specs=(pl.BlockSpec((tm, WIDTH), lambda i, j: (i, j)),
                   pl.BlockSpec((tm, LANE), lambda i, j: (i, 0))),
        scratch_shapes=[pltpu.VMEM((tm, d_model), bf16)],
        compiler_params=pltpu.CompilerParams(
            dimension_semantics=("arbitrary", "arbitrary"), vmem_limit_bytes=VMEM_LIMIT_BYTES),
        name="proj",
    )(x2d, gain, w_main, w_small, cos, sin_a, sin_b, fbias)


def _vproj_kernel(x_ref, gain_ref, wt_ref, vt_ref, h_ref):
    @pl.when(pl.program_id(1) == 0)
    def _():
        h_ref[...] = _rms_norm_bf16(x_ref, gain_ref)

    vt_ref[...] = _nt_dot(wt_ref[...], h_ref[...]).astype(bf16)


def _project_values(x2d, gain, w_vt, seq):
    n_rows, d_model = x2d.shape
    tm = min(PROJ_TM, seq)
    return pl.pallas_call(
        _vproj_kernel,
        out_shape=jax.ShapeDtypeStruct((N_BRANCH * WIDTH, n_rows), bf16),
        grid=(n_rows // tm, N_BRANCH),
        in_specs=[
            pl.BlockSpec((tm, d_model), lambda i, j: (i, 0)),
            pl.BlockSpec((1, d_model), lambda i, j: (0, 0)),
            pl.BlockSpec((WIDTH, d_model), lambda i, j: (j, 0)),
        ],
        out_specs=pl.BlockSpec((WIDTH, tm), lambda i, j: (j, i)),
        scratch_shapes=[pltpu.VMEM((tm, d_model), bf16)],
        compiler_params=pltpu.CompilerParams(
            dimension_semantics=("arbitrary", "arbitrary"), vmem_limit_bytes=VMEM_LIMIT_BYTES),
        name="vproj",
    )(x2d, gain, w_vt)


def _cumsum_kernel(x_ref, e_ref, carry_ref):
    @pl.when(pl.program_id(1) == 0)
    def _():
        carry_ref[...] = jnp.zeros_like(carry_ref)

    x = x_ref[...]
    ch = x.shape[0]
    tri = (lax.broadcasted_iota(jnp.int32, (ch, ch), 1)
           <= lax.broadcasted_iota(jnp.int32, (ch, ch), 0)).astype(bf16)
    cs = sum(jnp.dot(tri, piece, preferred_element_type=f32) for piece in _split3(x)) + carry_ref[0:1, :]
    carry_ref[...] = jnp.broadcast_to(cs[ch - 1:ch, :], carry_ref.shape)

    src = lax.broadcasted_iota(jnp.int32, (LANE, WIDTH), 0) - SMALL_F
    dst = lax.broadcasted_iota(jnp.int32, (LANE, WIDTH), 1)
    is_head = (src >= 0) & (src < N_HEADS)
    base = (src >> 1) * LANE + (src & 1) * N_PIECES
    out = None
    for n, piece in enumerate(_split3(cs * -LOG2E)):
        place = jnp.where(is_head & (dst == base + n), 1.0, 0.0).astype(bf16)
        term = jnp.dot(piece, place, preferred_element_type=f32)
        out = term if out is None else out + term
    e_ref[...] = out.astype(bf16)


def _cumsum_extras(small3d):
    b, s, _ = small3d.shape
    ch = min(CSUM_CHUNK, s)
    return pl.pallas_call(
        _cumsum_kernel,
        out_shape=jax.ShapeDtypeStruct((b, s, WIDTH), bf16),
        grid=(b, s // ch),
        in_specs=[pl.BlockSpec((None, ch, LANE), lambda i, c: (i, c, 0))],
        out_specs=pl.BlockSpec((None, ch, WIDTH), lambda i, c: (i, c, 0)),
        scratch_shapes=[pltpu.VMEM((SUBLANE, LANE), f32)],
        compiler_params=pltpu.CompilerParams(
            dimension_semantics=("arbitrary", "arbitrary"), vmem_limit_bytes=VMEM_LIMIT_BYTES),
        name="csum",
    )(small3d)


def _slab(h):
    return slice((h // 2) * LANE, (h // 2 + 1) * LANE)


def _queries_t(q_ref):
    row = lax.broadcasted_iota(jnp.int32, (LANE, TQ), 0)
    out = []
    for h in range(N_HEADS):
        if h % 2 == 0:
            qt = q_ref[:, _slab(h)].astype(f32).T
        first = HEAD_DIM * (h % 2)
        out.append(jnp.where((row >= first) & (row < first + HEAD_DIM), qt, 0.0).astype(bf16))
    return out


def _values_with_ones(vt_ref, h, off):
    return jnp.concatenate([vt_ref[h * HEAD_DIM:(h + 1) * HEAD_DIM, pl.ds(off, TK)],
                            jnp.ones((BF16_ROWS, TK), bf16)], axis=0)


def _online_update(s, m, vt, acc_ref, h, picked=None):
    mx = jnp.max(s, axis=0, keepdims=True)
    if picked is not None:
        mx = jnp.where(picked, mx, NEG)
    m_new = jnp.maximum(m, mx)
    shift = m_new if picked is None else jnp.where(picked, m_new, -NEG)
    p = jnp.exp2(s - shift).astype(bf16)
    acc_ref[h] = jnp.exp2(m - m_new) * acc_ref[h] + jnp.dot(vt, p, preferred_element_type=f32)
    return m_new


def _stale_shift_update(s, c, vt, acc_ref, h, picked=None):
    mx = jnp.max(s, axis=0, keepdims=True)
    if picked is not None:
        mx = jnp.where(picked, mx, NEG)
    shift = c if picked is None else jnp.where(picked, c, -NEG)
    p = jnp.exp2(s - shift).astype(bf16)
    c_new = jnp.maximum(c, mx)
    acc_ref[h] = jnp.exp2(c - c_new) * (acc_ref[h] + jnp.dot(vt, p, preferred_element_type=f32))
    return c_new, mx - c


def _write_output(acc_ref, o_ref):
    for pair in range(N_HEADS // 2):
        halves = []
        for h in (2 * pair, 2 * pair + 1):
            a = acc_ref[h]
            halves.append(a[:HEAD_DIM] / a[HEAD_DIM:HEAD_DIM + 1])
        o_ref[:, pair * LANE:(pair + 1) * LANE] = jnp.concatenate(halves, axis=0).T.astype(bf16)


def _diag_keep():
    return lax.broadcasted_iota(jnp.int32, (TK, TQ), 0) <= lax.broadcasted_iota(jnp.int32, (TK, TQ), 1)


def _init_rows():
    return tuple(jnp.full((1, TQ), NEG, f32) for _ in range(N_HEADS))


def _for_key_tiles(start, n, scores, update, carry):
    def pair(i, c):
        j = start + 2 * i
        s_a, s_b = scores(j), scores(j + 1)
        return update(j + 1, s_b, update(j, s_a, c))

    carry = lax.fori_loop(0, lax.shift_right_logical(n, 1), pair, carry)
    last = start + n - 1
    return lax.cond((n & 1) == 1, lambda c: update(last, scores(last), c), lambda c: c, carry)


def _make_updates(masked, vt_ref, acc_ref):
    def exact(j, sc, ms, diagonal=False):
        off = pl.multiple_of(j * TK, TK)
        out = []
        for h, s in enumerate(sc):
            s, picked = masked(j, h, s, diagonal)
            out.append(_online_update(s, ms[h], _values_with_ones(vt_ref, h, off), acc_ref, h, picked))
        return tuple(out)

    def one_pass(j, sc, carry):
        off = pl.multiple_of(j * TK, TK)
        rise, out = carry[N_HEADS], []
        for h, s in enumerate(sc):
            s, picked = masked(j, h, s, False)
            c_new, up = _stale_shift_update(s, carry[h], _values_with_ones(vt_ref, h, off), acc_ref, h, picked)
            out.append(c_new)
            rise = jnp.maximum(rise, up)
        return tuple(out) + (rise,)

    return exact, one_pass


def _attend(first, start, n, scores, masked, vt_ref, acc_ref, o_ref):
    exact, one_pass = _make_updates(masked, vt_ref, acc_ref)

    def run(update, tail):
        acc_ref[...] = jnp.zeros_like(acc_ref)
        ms = exact(first, scores(first), _init_rows(), diagonal=True)
        return _for_key_tiles(start, n, scores, update, ms + tail)

    rise = run(one_pass, (jnp.full((1, TQ), NEG, f32),))[N_HEADS]

    @pl.when(jnp.logical_not(jnp.max(rise) <= STALE_SHIFT_LIMIT))
    def _():
        run(exact, ())

    _write_output(acc_ref, o_ref)


def _fox_kernel(q_ref, k_ref, e_ref, vt_ref, o_ref, acc_ref):
    t = pl.program_id(1)
    keep = _diag_keep()
    row = lax.broadcasted_iota(jnp.int32, (LANE, TQ), 0)
    q_aug = []
    for h, qt in enumerate(_queries_t(q_ref)):
        first = (h % 2) * N_PIECES
        ones = jnp.where((row >= first) & (row < first + N_PIECES), 1.0, 0.0).astype(bf16)
        q_aug.append(jnp.concatenate([qt, ones], axis=0))

    def scores(j):
        off = pl.multiple_of(j * TK, TK)
        out = []
        for h in range(N_HEADS):
            if h % 2 == 0:
                k_aug = jnp.concatenate([k_ref[pl.ds(off, TK), _slab(h)], e_ref[pl.ds(off, TK), _slab(h)]], axis=1)
            out.append(jnp.dot(k_aug, q_aug[h], preferred_element_type=f32))
        return out

    def masked(j, h, s, diagonal):
        return (jnp.where(keep, s, NEG) if diagonal else s), None

    _attend(t, 0, t, scores, masked, vt_ref, acc_ref, o_ref)


def _fox(p3d, extras, vt):
    b, s, _ = p3d.shape
    return pl.pallas_call(
        _fox_kernel,
        out_shape=jax.ShapeDtypeStruct((b, s, WIDTH), bf16),
        grid=(b, s // TQ),
        in_specs=[
            pl.BlockSpec((None, TQ, WIDTH), lambda i, t: (i, t, SEC_QC)),
            pl.BlockSpec((None, s, WIDTH), lambda i, t: (i, 0, SEC_KC)),
            pl.BlockSpec((None, s, WIDTH), lambda i, t: (i, 0, 0)),
            pl.BlockSpec((WIDTH, s), lambda i, t: (2, i)),
        ],
        out_specs=pl.BlockSpec((None, TQ, WIDTH), lambda i, t: (i, t, 0)),
        scratch_shapes=[pltpu.VMEM((N_HEADS, ACC_ROWS, TQ), f32)],
        compiler_params=pltpu.CompilerParams(
            dimension_semantics=("arbitrary", "arbitrary"), vmem_limit_bytes=VMEM_LIMIT_BYTES),
        name="fox",
    )(p3d, p3d, extras, vt)


def _moba_kernel(q_ref, k_ref, vt_ref, o_ref, acc_ref, km_ref, sel_ref, *, n_blk):
    t = pl.program_id(1)
    nb_pad = km_ref.shape[0]

    @pl.when(t == 0)
    def _():
        km_ref[...] = jnp.zeros_like(km_ref)
        for blk in range(n_blk):
            kb = k_ref[blk * MOBA_BLOCK:(blk + 1) * MOBA_BLOCK, :].astype(f32)
            km_ref[blk:blk + 1, :] = jnp.sum(kb, axis=0, keepdims=True) * (1.0 / MOBA_BLOCK)

    keep = _diag_keep()
    blk_row =lax.broadcasted_iota(jnp.int32, (nb_pad, TQ), 0)
    blk_row_f = blk_row.astype(f32)
    qt = _queries_t(q_ref)

    for h in range(N_HEADS):
        gate = jnp.dot(km_ref[:, _slab(h)].astype(bf16), qt[h], preferred_element_type=f32)
        g = jnp.where(blk_row < t, gate, NEG)
        sel = jnp.zeros((nb_pad, TQ), f32)
        for _ in range(min(MOBA_TOPK, n_blk - 1)):
            mx = jnp.max(g, axis=0, keepdims=True)
            first = jnp.min(jnp.where(g == mx, blk_row_f, float(nb_pad)), axis=0, keepdims=True)
            hit = blk_row_f == first
            sel = jnp.where(hit, jnp.where(mx > 0.5 * NEG, 1.0, sel), sel)
            g = jnp.where(hit, NEG, g)
        sel_ref[h] = sel

    def scores(j):
        off = pl.multiple_of(j * TK, TK)
        return [jnp.dot(k_ref[pl.ds(off, TK), _slab(h)], qt[h], preferred_element_type=f32)
                for h in range(N_HEADS)]

    def masked(j, h, s, diagonal):
        if diagonal:
            return jnp.where(keep, s, NEG), None
        return s, sel_ref[h, pl.ds(j, 1), :] > 0.5

    _attend(t, 0, t, scores, masked, vt_ref, acc_ref, o_ref)


def _moba(p3d, vt):
    b, s, _ = p3d.shape
    assert TQ == MOBA_BLOCK and TK == MOBA_BLOCK and s % MOBA_BLOCK == 0
    n_blk = s // MOBA_BLOCK
    nb_pad = -(-n_blk // BF16_ROWS) * BF16_ROWS
    return pl.pallas_call(
        functools.partial(_moba_kernel, n_blk=n_blk),
        out_shape=jax.ShapeDtypeStruct((b, s, WIDTH), bf16),
        grid=(b, n_blk),
        in_specs=[
            pl.BlockSpec((None, TQ, WIDTH), lambda i, t: (i, t, SEC_QA)),
            pl.BlockSpec((None, s, WIDTH), lambda i, t: (i, 0, SEC_KA)),
            pl.BlockSpec((WIDTH, s), lambda i, t: (0, i)),
        ],
        out_specs=pl.BlockSpec((None, TQ, WIDTH), lambda i, t: (i, t, 0)),
        scratch_shapes=[pltpu.VMEM((N_HEADS, ACC_ROWS, TQ), f32),
                        pltpu.VMEM((nb_pad, WIDTH), f32),
                        pltpu.VMEM((N_HEADS, nb_pad, TQ), f32)],
        compiler_params=pltpu.CompilerParams(
            dimension_semantics=("arbitrary", "arbitrary"), vmem_limit_bytes=VMEM_LIMIT_BYTES),
        name="moba",
    )(p3d, p3d, vt)


def _dsa_kernel(q_ref, k_ref, vt_ref, qi_ref, kk_ref, sm_ref, o_ref, acc_ref, sc_ref, sc16_ref, *, top, seq):
    t = pl.program_id(1)
    n_kv = t + 1
    n_chunks = lax.shift_right_logical(n_kv + 1, 1)
    n_keys = (n_kv * TK).astype(f32)
    k_top = float(top)
    keep = _diag_keep()
    chunk_row = lax.broadcasted_iota(jnp.int32, (SEARCH_CHUNK, TQ), 0)
    group_row = lax.broadcasted_iota(jnp.int32, (SUBLANE, TQ), 0)
    w_rows = sm_ref[...].T
    qit = _queries_t(qi_ref)

    @pl.when((n_kv & 1) == 1)
    def _():
        pad = pl.ds(pl.multiple_of(n_kv * TK, TK), TK)
        sc_ref[pad, :] = jnp.full((TK, TQ), -jnp.inf, f32)
        sc16_ref[pad, :] = jnp.full((TK, TQ), -jnp.inf, bf16)

    def score_body(j, _):
        off = pl.multiple_of(j * TK, TK)
        kkb = kk_ref[pl.ds(off, TK), :]
        rel = [jnp.dot(kkb, qit[hh], preferred_element_type=f32) for hh in range(IDX_HEADS)]
        isc = jnp.zeros((TK, TQ), f32)
        for hh in range(IDX_HEADS):
            isc = isc + w_rows[SMALL_WIDX + hh:SMALL_WIDX + hh + 1, :] * jnp.maximum(rel[hh], 0.0)
        isc = jnp.where((j < t) | keep, isc, -jnp.inf)
        sc_ref[pl.ds(off, TK), :] = isc
        sc16_ref[pl.ds(off, TK), :] = isc.astype(bf16)
        return 0

    lax.fori_loop(0, n_kv, score_body, 0)

    def count_in(ref, rows_per_vreg, one, pred):
        def body(c, accs):
            off = pl.multiple_of(c * SEARCH_CHUNK, SEARCH_CHUNK)
            chunk = ref[pl.ds(off, SEARCH_CHUNK), :]
            accs = list(accs)
            for g in range(SEARCH_CHUNK // rows_per_vreg):
                rows = chunk[g * rows_per_vreg:(g + 1) * rows_per_vreg, :]
                hit = jnp.where(pred(rows, off + g * rows_per_vreg), one, one * 0)
                accs[g % COUNT_CHAINS] = accs[g % COUNT_CHAINS] + hit
            return tuple(accs)
        init = tuple(jnp.zeros((rows_per_vreg, TQ), one.dtype) for _ in range(COUNT_CHAINS))
        accs = lax.fori_loop(0, n_chunks, body, init)
        return jnp.sum(sum(a.astype(f32) for a in accs), axis=0, keepdims=True)

    def count(pred):
        return count_in(sc_ref, SUBLANE, jnp.float32(1), pred)

    def count16(pred):
        return count_in(sc16_ref, BF16_ROWS, jnp.bfloat16(1), pred)

    def as_score(code):
        return pltpu.bitcast(jnp.where(code < 0, code ^ 0x7FFFFFFF, code), f32)

    def as_score16(code16):
        return pltpu.bitcast(jnp.left_shift(jnp.where(code16 < 0, code16 ^ 0x7FFF, code16), 16), f32)

    def search(n_bits, first, value_of, count_ge, floor_count):
        def bit_body(i, carry):
            code, n_ge = carry
            cand = code + jnp.left_shift(jnp.int32(1), (n_bits - 1) - i)
            cand_f = value_of(cand)
            c = count_ge(cand_f)
            ok = (c >= k_top) & (cand_f > -jnp.inf)
            return jnp.where(ok, cand, code), jnp.where(ok, c, n_ge)
        return lax.fori_loop(0, n_bits, bit_body, (first, floor_count))

    i16_min = jnp.full((1, TQ), I16_MIN, jnp.int32)
    code16, _ = search(16, i16_min, as_score16, lambda v: count16(lambda r, _: r >= v.astype(bf16)),
                       jnp.broadcast_to(n_keys, (1, TQ)))
    has_thr = code16 > I16_MIN
    centre = code16 * 65536 + jnp.where(code16 < 0, 65535, 0)
    first = jnp.where(has_thr, centre - 65536, INT_MIN)
    n_first = jnp.where(has_thr, count(lambda sc, _: sc >= as_score(first)), n_keys)
    code, n_ge = search(17, first, as_score, lambda v: count(lambda sc, _: sc >= v), n_first)
    thr = jnp.where(has_thr, as_score(code), -jnp.inf)

    no_cut = jnp.where(has_thr, seq + 1, 0).astype(jnp.int32)
    over = jnp.max(jnp.where(has_thr & (n_ge > k_top), 1.0, 0.0))

    def tie_cut():
        need = k_top - count(lambda sc, off: sc > thr)
        n_bits = seq.bit_length()

        def cut_body(i, cut):
            cand = cut + jnp.left_shift(jnp.int32(1), (n_bits - 1) - i)
            c = count(lambda sc, first: (sc == thr) & (group_row + first < cand))
            return jnp.where(c <= need, cand, cut)

        cut = lax.fori_loop(0, n_bits, cut_body, jnp.zeros((1, TQ), jnp.int32))
        return jnp.where(has_thr, cut, 0)

    cut = lax.cond(over > 0.5, tie_cut, lambda: no_cut)

    def ceiling_body(c, _):
        off = pl.multiple_of(c * SEARCH_CHUNK, SEARCH_CHUNK)
        sc = sc_ref[pl.ds(off, SEARCH_CHUNK), :]
        chosen = (sc > thr) | ((sc == thr) & (chunk_row + off < cut))
        sc_ref[pl.ds(off, SEARCH_CHUNK), :] = jnp.where(chosen, -NEG, NEG).astype(f32)
        return 0

    lax.fori_loop(0, n_chunks, ceiling_body, 0)

    qt = _queries_t(q_ref)

    def scores(j):
        off = pl.multiple_of(j * TK, TK)
        return [jnp.dot(k_ref[pl.ds(off, TK), _slab(h)], qt[h], preferred_element_type=f32)
                for h in range(N_HEADS)]

    def masked(j, h, s, diagonal):
        ceiling = sc_ref[pl.ds(pl.multiple_of(j * TK, TK), TK), :]
        return jnp.minimum(s, ceiling), None

    _attend(0, 1, t, scores, masked, vt_ref, acc_ref, o_ref)


def _dsa(p3d, vt, kk, small3d):
    b, s, _ = p3d.shape
    top = min(DSA_TOPK_MAX, s // 4)
    return pl.pallas_call(
        functools.partial(_dsa_kernel, top=top, seq=s),
        out_shape=jax.ShapeDtypeStruct((b, s, WIDTH), bf16),
        grid=(b, s // TQ),
        in_specs=[
            pl.BlockSpec((None, TQ, WIDTH), lambda i, t: (i, t, SEC_QB)),
            pl.BlockSpec((None, s, WIDTH), lambda i, t: (i, 0, SEC_KB)),
            pl.BlockSpec((WIDTH, s), lambda i, t: (1, i)),
            pl.BlockSpec((None, TQ, WIDTH), lambda i, t: (i, t, SEC_QIDX)),
            pl.BlockSpec((None, s, LANE), lambda i, t: (i, 0, 0)),
            pl.BlockSpec((None, TQ, LANE), lambda i, t: (i, t, 0)),
        ],
        out_specs=pl.BlockSpec((None, TQ, WIDTH), lambda i, t: (i, t, 0)),
        scratch_shapes=[pltpu.VMEM((N_HEADS, ACC_ROWS, TQ), f32), pltpu.VMEM((s, TQ), f32), pltpu.VMEM((s, TQ), bf16)],
        compiler_params=pltpu.CompilerParams(
            dimension_semantics=("arbitrary", "arbitrary"), vmem_limit_bytes=VMEM_LIMIT_BYTES),
        name="dsa",
    )(p3d, p3d, vt, p3d, kk, small3d)


def _out_kernel(oa_ref, ob_ref, oc_ref, ga_ref, gb_ref, gc_ref, m0_ref, m1_ref, m2_ref,
                x_ref, wb_ref, wo_ref, fg_ref, y_ref, *, final_norm):
    merged = None
    for n, (o_ref, g_ref, mg_ref) in enumerate(((oa_ref, ga_ref, m0_ref), (ob_ref, gb_ref, m1_ref),
                                                (oc_ref, gc_ref, m2_ref))):
        gated = (o_ref[...].astype(f32) * g_ref[...].astype(f32)).astype(bf16)
        y = jnp.dot(gated, wb_ref[n], preferred_element_type=f32) * mg_ref[...].astype(f32)
        merged = y if merged is None else merged + y
    out = x_ref[...] + jnp.dot(merged.astype(bf16), wo_ref[...], preferred_element_type=f32)
    if final_norm:
        ms = jnp.mean(out * out, axis=-1, keepdims=True)
        out = out * lax.rsqrt(ms + RMS_EPS) * fg_ref[...]
    y_ref[...] = out


def _output(oa, ob, oc, p2d, x2d, w_branch, w_out, final_gain, final_norm):
    n_rows, d_model = x2d.shape
    tm = min(OUT_TM, n_rows)
    assert SEC_MERGE * WIDTH % d_model == 0
    mg0 = SEC_MERGE * WIDTH // d_model
    row = lambda c: (lambda i: (i, c))
    return pl.pallas_call(
        functools.partial(_out_kernel, final_norm=final_norm),
        out_shape=jax.ShapeDtypeStruct((n_rows, d_model), f32),
        grid=(n_rows // tm,),
        in_specs=[
            pl.BlockSpec((tm, WIDTH), row(0)), pl.BlockSpec((tm, WIDTH), row(0)), pl.BlockSpec((tm, WIDTH), row(0)),
            pl.BlockSpec((tm, WIDTH), row(SEC_GA)), pl.BlockSpec((tm, WIDTH), row(SEC_GB)),
            pl.BlockSpec((tm, WIDTH), row(SEC_GC)),
            pl.BlockSpec((tm, d_model), row(mg0)), pl.BlockSpec((tm, d_model), row(mg0 + 1)),
            pl.BlockSpec((tm, d_model), row(mg0 + 2)),
            pl.BlockSpec((tm, d_model), row(0)),
            pl.BlockSpec((N_BRANCH, WIDTH, d_model), lambda i: (0, 0, 0)),
            pl.BlockSpec((d_model, d_model), lambda i: (0, 0)),
            pl.BlockSpec((1, d_model), lambda i: (0, 0)),
        ],
        out_specs=pl.BlockSpec((tm, d_model), row(0)),
        compiler_params=pltpu.CompilerParams(
            dimension_semantics=("arbitrary",), vmem_limit_bytes=VMEM_LIMIT_BYTES),
        name="out",
    )(oa, ob, oc, p2d, p2d, p2d, p2d, p2d, p2d, x2d, w_branch, w_out, final_gain)


def _pack_w_in(w):
    sec = lambda o: w[:, o * WIDTH:(o + 1) * WIDTH]
    small0 = 12 * WIDTH + IDX_HEADS * IDX_DIM
    merge0 = small0 + IDX_DIM + IDX_HEADS + N_HEADS
    order = (0, 1, 4, 5, 12, 8, 9, 3, 7, 11)
    main = jnp.concatenate([sec(o) for o in order] + [w[:, merge0:]], axis=1)
    small = jnp.pad(w[:, small0:merge0], ((0, 0), (0, LANE - (merge0 - small0))))
    values_t = jnp.concatenate([sec(2), sec(6), sec(10)], axis=1).T
    return main.astype(bf16), small.astype(bf16), values_t.astype(bf16)


def _rope_tables(seq):
    half = ROPE_DIM // 2
    inv_freq = jnp.power(jnp.float32(ROPE_THETA), -jnp.arange(0, ROPE_DIM, 2, dtype=f32) / ROPE_DIM)
    ang = jnp.arange(seq).astype(f32)[:, None] * inv_freq[None, :]
    cos, sin = jnp.cos(ang), jnp.sin(ang)
    pad = jnp.zeros((seq, HEAD_DIM - ROPE_DIM), f32)
    zero = jnp.zeros((seq, half), f32)
    tile = lambda a: jnp.tile(a, (1, LANE // HEAD_DIM))
    cos_t = tile(jnp.concatenate([cos, cos, pad + 1.0], axis=1))
    sin_a = tile(jnp.concatenate([-sin, zero, pad], axis=1))
    sin_b = tile(jnp.concatenate([zero, sin, pad], axis=1))
    return cos_t, sin_a, sin_b


def kernel(x, norm_gain, w_in, forget_bias, w_branch, w_out, final_gain):
    b, s, d_model = x.shape
    depth = norm_gain.shape[0]
    assert s % TQ == 0 and s % min(PROJ_TM, s) == 0 and s % min(CSUM_CHUNK, s) == 0
    cos_t, sin_a, sin_b = _rope_tables(s)
    h2d = x.reshape(b * s, d_model)
    for layer in range(depth):
        w_main, w_small, w_vt = _pack_w_in(w_in[layer])
        gain = norm_gain[layer][None, :]
        fbias = jnp.zeros((1, LANE), f32).at[0, SMALL_F:SMALL_F + N_HEADS].set(forget_bias[layer])
        p2d, small2d = _project(h2d, gain, w_main, w_small, cos_t, sin_a, sin_b, fbias, s)
        vt = _project_values(h2d, gain, w_vt, s)
        p3d = p2d.reshape(b, s, N_SEC * WIDTH)
        small3d = small2d.reshape(b, s, LANE)
        extras = _cumsum_extras(small3d)
        k_idx = small3d[:, :, SMALL_KIDX:SMALL_KIDX + IDX_DIM].astype(bf16)
        kk = jnp.concatenate([k_idx, k_idx], axis=-1)
        oa = _moba(p3d, vt)
        ob = _dsa(p3d, vt, kk, small3d)
        oc = _fox(p3d, extras, vt)
        h2d = _output(oa.reshape(b * s, WIDTH), ob.reshape(b * s, WIDTH), oc.reshape(b * s, WIDTH),
                      p2d, h2d, w_branch[layer].astype(bf16), w_out[layer].astype(bf16),
                      final_gain[None, :], layer == depth - 1)
    return h2d.reshape(b, s, d_model)
```

```python
import functools

import jax
import jax.numpy as jnp
from jax import lax
from jax.experimental import pallas as pl
from jax.experimental.pallas import tpu as pltpu

HEAD_DIM = 64
N_HEADS = 8
WIDTH = N_HEADS * HEAD_DIM
N_BRANCH = 3
ROPE_DIM = HEAD_DIM // 4
ROPE_THETA = 500000.0
MOBA_BLOCK = 256
MOBA_TOPK = 3
IDX_HEADS = 8
IDX_DIM = 64
DSA_TOPK_MAX = 256
RMS_EPS = 1e-6
LOG2E = 1.4426950408889634
ATTN_SCALE = HEAD_DIM ** -0.5 * LOG2E
IDX_SCALE = IDX_HEADS ** -0.5 * IDX_DIM ** -0.5

LANE = 128
SUBLANE = 8
BF16_ROWS = 16
VMEM_LIMIT_BYTES = 48 * 1024 * 1024

TQ = 256
TK = 256
PROJ_TM = 2048
PROJ_SUB = 256
OUT_TM = 1024
CSUM_CHUNK = 512
ACC_ROWS = HEAD_DIM + BF16_ROWS

NEG = -1e30
STALE_SHIFT_LIMIT = 64.0
INT_MIN = -(2 ** 31)
I16_MIN = -(2 ** 15)
SEARCH_CHUNK = 2 * TK
COUNT_CHAINS = 2

SEC_QA, SEC_KA, SEC_QB, SEC_KB, SEC_QIDX = 0, 1, 2, 3, 4
SEC_QC, SEC_KC = 5, 6
SEC_GA, SEC_GB, SEC_GC = 7, 8, 9
SEC_MERGE = 10
N_SEC = 16
SMALL_KIDX, SMALL_WIDX, SMALL_F = 0, 64, 72
N_PIECES = 3

f32 = jnp.float32
bf16 = jnp.bfloat16


def _nt_dot(a, b):
    return lax.dot_general(a, b, (((1,), (1,)), ((), ())), preferred_element_type=f32)


def _split3(x):
    hi = x.astype(bf16)
    r1 = x - hi.astype(f32)
    mid = r1.astype(bf16)
    lo = (r1 - mid.astype(f32)).astype(bf16)
    return hi, mid, lo


def _rope(x, cos, sin_a, sin_b):
    return x * cos + pltpu.roll(x, LANE - ROPE_DIM // 2, 1) * sin_a + pltpu.roll(x, ROPE_DIM // 2, 1) * sin_b


def _rms_norm_bf16(x_ref, gain_ref):
    xf = x_ref[...]
    ms = jnp.mean(xf * xf, axis=-1, keepdims=True)
    return (xf * lax.rsqrt(ms + RMS_EPS) * gain_ref[...]).astype(bf16)


def _proj_kernel(x_ref, gain_ref, w_ref, ws_ref, cos_ref, sa_ref, sb_ref, fb_ref,
                 p_ref, small_ref, h_ref):
    j = pl.program_id(1)

    @pl.when(j == 0)
    def _():
        h_ref[...] = _rms_norm_bf16(x_ref, gain_ref)
        sm = jnp.dot(h_ref[...], ws_ref[...], preferred_element_type=f32)
        lane = lax.broadcasted_iota(jnp.int32, sm.shape, 1)
        z = sm + fb_ref[...]
        log_f = jnp.minimum(z, 0.0) - jnp.log1p(jnp.exp(-jnp.abs(z)))
        small_ref[...] = jnp.where(lane < SMALL_WIDX, _rope(sm, cos_ref[...], sa_ref[...], sb_ref[...]),
                                   jnp.where(lane < SMALL_F, sm * IDX_SCALE, log_f))

    def row_blocks(epilogue):
        for r in range(h_ref.shape[0] // PROJ_SUB):
            rows = slice(r * PROJ_SUB, (r + 1) * PROJ_SUB)
            acc = jnp.dot(h_ref[rows, :], w_ref[...], preferred_element_type=f32)
            p_ref[rows, :] = epilogue(acc, rows).astype(bf16)

    def rope_all(acc, rows):
        return jnp.concatenate(
            [_rope(acc[:, c * LANE:(c + 1) * LANE], cos_ref[rows, :], sa_ref[rows, :], sb_ref[rows, :])
             for c in range(WIDTH // LANE)], axis=1)

    @pl.when(j <= SEC_QIDX)
    def _():
        scale = jnp.where((j == SEC_QA) | (j == SEC_QB), ATTN_SCALE, 1.0).astype(f32)
        row_blocks(lambda acc, rows: rope_all(acc, rows) * scale)

    @pl.when((j > SEC_QIDX) & (j < SEC_GA))
    def _():
        scale = jnp.where(j == SEC_QC, ATTN_SCALE, 1.0).astype(f32)
        row_blocks(lambda acc, rows: acc * scale)

    @pl.when((j >= SEC_GA) & (j < SEC_MERGE))
    def _():
        row_blocks(lambda acc, rows: acc * jax.nn.sigmoid(acc))

    @pl.when(j >= SEC_MERGE)
    def _():
        row_blocks(lambda acc, rows: jax.nn.sigmoid(acc))


def _project(x2d, gain, w_main, w_small, cos, sin_a, sin_b, fbias, seq):
    n_rows, d_model = x2d.shape
    tm = min(PROJ_TM, seq)
    n_seq_tiles = seq // tm
    return pl.pallas_call(
        _proj_kernel,
        out_shape=(jax.ShapeDtypeStruct((n_rows, N_SEC * WIDTH), bf16),
                   jax.ShapeDtypeStruct((n_rows, LANE), f32)),
        grid=(n_rows // tm, N_SEC),
        in_specs=[
            pl.BlockSpec((tm, d_model), lambda i, j: (i, 0)),
            pl.BlockSpec((1, d_model), lambda i, j: (0, 0)),
            pl.BlockSpec((d_model, WIDTH), lambda i, j: (0, j)),
            pl.BlockSpec((d_model, LANE), lambda i, j: (0, 0)),
            pl.BlockSpec((tm, LANE), lambda i, j: (i % n_seq_tiles, 0)),
            pl.BlockSpec((tm, LANE), lambda i, j: (i % n_seq_tiles, 0)),
            pl.BlockSpec((tm, LANE), lambda i, j: (i % n_seq_tiles, 0)),
            pl.BlockSpec((1, LANE), lambda i, j: (0, 0)),
        ],
        out_specs=(pl.BlockSpec((tm, WIDTH), lambda i, j: (i, j)),
                   pl.BlockSpec((tm, LANE), lambda i, j: (i, 0))),
        scratch_shapes=[pltpu.VMEM((tm, d_model), bf16)],
        compiler_params=pltpu.CompilerParams(
            dimension_semantics=("arbitrary", "arbitrary"), vmem_limit_bytes=VMEM_LIMIT_BYTES),
        name="proj",
    )(x2d, gain, w_main, w_small, cos, sin_a, sin_b, fbias)


def _vproj_kernel(x_ref, gain_ref, wt_ref, vt_ref, h_ref):
    @pl.when(pl.program_id(1) == 0)
    def _():
        h_ref[...] = _rms_norm_bf16(x_ref, gain_ref)

    vt_ref[...] = _nt_dot(wt_ref[...], h_ref[...]).astype(bf16)


def _project_values(x2d, gain, w_vt, seq):
    n_rows, d_model = x2d.shape
    tm = min(PROJ_TM, seq)
    return pl.pallas_call(
        _vproj_kernel,
        out_shape=jax.ShapeDtypeStruct((N_BRANCH * WIDTH, n_rows), bf16),
        grid=(n_rows // tm, N_BRANCH),
        in_specs=[
            pl.BlockSpec((tm, d_model), lambda i, j: (i, 0)),
            pl.BlockSpec((1, d_model), lambda i, j: (0, 0)),
            pl.BlockSpec((WIDTH, d_model), lambda i, j: (j, 0)),
        ],
        out_specs=pl.BlockSpec((WIDTH, tm), lambda i, j: (j, i)),
        scratch_shapes=[pltpu.VMEM((tm, d_model), bf16)],
        compiler_params=pltpu.CompilerParams(
            dimension_semantics=("arbitrary", "arbitrary"), vmem_limit_bytes=VMEM_LIMIT_BYTES),
        name="vproj",
    )(x2d, gain, w_vt)


def _cumsum_kernel(x_ref, e_ref, carry_ref):
    @pl.when(pl.program_id(1) == 0)
    def _():
        carry_ref[...] = jnp.zeros_like(carry_ref)

    x = x_ref[...]
    ch = x.shape[0]
    tri = (lax.broadcasted_iota(jnp.int32, (ch, ch), 1)
           <= lax.broadcasted_iota(jnp.int32, (ch, ch), 0)).astype(bf16)
    cs = sum(jnp.dot(tri, piece, preferred_element_type=f32) for piece in _split3(x)) + carry_ref[0:1, :]
    carry_ref[...] = jnp.broadcast_to(cs[ch - 1:ch, :], carry_ref.shape)

    src = lax.broadcasted_iota(jnp.int32, (LANE, WIDTH), 0) - SMALL_F
    dst = lax.broadcasted_iota(jnp.int32, (LANE, WIDTH), 1)
    is_head = (src >= 0) & (src < N_HEADS)
    base = (src >> 1) * LANE + (src & 1) * N_PIECES
    out = None
    for n, piece in enumerate(_split3(cs * -LOG2E)):
        place = jnp.where(is_head & (dst == base + n), 1.0, 0.0).astype(bf16)
        term = jnp.dot(piece, place, preferred_element_type=f32)
        out = term if out is None else out + term
    e_ref[...] = out.astype(bf16)


def _cumsum_extras(small3d):
    b, s, _ = small3d.shape
    ch = min(CSUM_CHUNK, s)
    return pl.pallas_call(
        _cumsum_kernel,
        out_shape=jax.ShapeDtypeStruct((b, s, WIDTH), bf16),
        grid=(b, s // ch),
        in_specs=[pl.BlockSpec((None, ch, LANE), lambda i, c: (i, c, 0))],
        out_specs=pl.BlockSpec((None, ch, WIDTH), lambda i, c: (i, c, 0)),
        scratch_shapes=[pltpu.VMEM((SUBLANE, LANE), f32)],
        compiler_params=pltpu.CompilerParams(
            dimension_semantics=("arbitrary", "arbitrary"), vmem_limit_bytes=VMEM_LIMIT_BYTES),
        name="csum",
    )(small3d)


def _slab(h):
    return slice((h // 2) * LANE, (h // 2 + 1) * LANE)


def _queries_t(q_ref):
    row = lax.broadcasted_iota(jnp.int32, (LANE, TQ), 0)
    out = []
    for h in range(N_HEADS):
        if h % 2 == 0:
            qt = q_ref[:, _slab(h)].astype(f32).T
        first = HEAD_DIM * (h % 2)
        out.append(jnp.where((row >= first) & (row < first + HEAD_DIM), qt, 0.0).astype(bf16))
    return out


def _values_with_ones(vt_ref, h, off):
    return jnp.concatenate([vt_ref[h * HEAD_DIM:(h + 1) * HEAD_DIM, pl.ds(off, TK)],
                            jnp.ones((BF16_ROWS, TK), bf16)], axis=0)


def _online_update(s, m, vt, acc_ref, h, picked=None):
    mx = jnp.max(s, axis=0, keepdims=True)
    if picked is not None:
        mx = jnp.where(picked, mx, NEG)
    m_new = jnp.maximum(m, mx)
    shift = m_new if picked is None else jnp.where(picked, m_new, -NEG)
    p = jnp.exp2(s - shift).astype(bf16)
    acc_ref[h] = jnp.exp2(m - m_new) * acc_ref[h] + jnp.dot(vt, p, preferred_element_type=f32)
    return m_new


def _stale_shift_update(s, c, vt, acc_ref, h, picked=None):
    mx = jnp.max(s, axis=0, keepdims=True)
    if picked is not None:
        mx = jnp.where(picked, mx, NEG)
    shift = c if picked is None else jnp.where(picked, c, -NEG)
    p = jnp.exp2(s - shift).astype(bf16)
    c_new = jnp.maximum(c, mx)
    acc_ref[h] = jnp.exp2(c - c_new) * (acc_ref[h] + jnp.dot(vt, p, preferred_element_type=f32))
    return c_new, mx - c


def _write_output(acc_ref, o_ref):
    for pair in range(N_HEADS // 2):
        halves = []
        for h in (2 * pair, 2 * pair + 1):
            a = acc_ref[h]
            halves.append(a[:HEAD_DIM] / a[HEAD_DIM:HEAD_DIM + 1])
        o_ref[:, pair * LANE:(pair + 1) * LANE] = jnp.concatenate(halves, axis=0).T.astype(bf16)


def _diag_keep():
    return lax.broadcasted_iota(jnp.int32, (TK, TQ), 0) <= lax.broadcasted_iota(jnp.int32, (TK, TQ), 1)


def _init_rows():
    return tuple(jnp.full((1, TQ), NEG, f32) for _ in range(N_HEADS))


def _for_key_tiles(start, n, scores, update, carry):
    def pair(i, c):
        j = start + 2 * i
        s_a, s_b = scores(j), scores(j + 1)
        return update(j + 1, s_b, update(j, s_a, c))

    carry = lax.fori_loop(0, lax.shift_right_logical(n, 1), pair, carry)
    last = start + n - 1
    return lax.cond((n & 1) == 1, lambda c: update(last, scores(last), c), lambda c: c, carry)


def _make_updates(masked, vt_ref, acc_ref):
    def exact(j, sc, ms, diagonal=False):
        off = pl.multiple_of(j * TK, TK)
        out = []
        for h, s in enumerate(sc):
            s, picked = masked(j, h, s, diagonal)
            out.append(_online_update(s, ms[h], _values_with_ones(vt_ref, h, off), acc_ref, h, picked))
        return tuple(out)

    def one_pass(j, sc, carry):
        off = pl.multiple_of(j * TK, TK)
        rise, out = carry[N_HEADS], []
        for h, s in enumerate(sc):
            s, picked = masked(j, h, s, False)
            c_new, up = _stale_shift_update(s, carry[h], _values_with_ones(vt_ref, h, off), acc_ref, h, picked)
            out.append(c_new)
            rise = jnp.maximum(rise, up)
        return tuple(out) + (rise,)

    return exact, one_pass


def _attend(first, start, n, scores, masked, vt_ref, acc_ref, o_ref):
    exact, one_pass = _make_updates(masked, vt_ref, acc_ref)

    def run(update, tail):
        acc_ref[...] = jnp.zeros_like(acc_ref)
        ms = exact(first, scores(first), _init_rows(), diagonal=True)
        return _for_key_tiles(start, n, scores, update, ms + tail)

    rise = run(one_pass, (jnp.full((1, TQ), NEG, f32),))[N_HEADS]

    @pl.when(jnp.logical_not(jnp.max(rise) <= STALE_SHIFT_LIMIT))
    def _():
        run(exact, ())

    _write_output(acc_ref, o_ref)


def _fox_kernel(q_ref, k_ref, e_ref, vt_ref, o_ref, acc_ref):
    t = pl.program_id(1)
    keep = _diag_keep()
    row = lax.broadcasted_iota(jnp.int32, (LANE, TQ), 0)
    q_aug = []
    for h, qt in enumerate(_queries_t(q_ref)):
        first = (h % 2) * N_PIECES
        ones = jnp.where((row >= first) & (row < first + N_PIECES), 1.0, 0.0).astype(bf16)
        q_aug.append(jnp.concatenate([qt, ones], axis=0))

    def scores(j):
        off = pl.multiple_of(j * TK, TK)
        out = []
        for h in range(N_HEADS):
            if h % 2 == 0:
                k_aug = jnp.concatenate([k_ref[pl.ds(off, TK), _slab(h)], e_ref[pl.ds(off, TK), _slab(h)]], axis=1)
            out.append(jnp.dot(k_aug, q_aug[h], preferred_element_type=f32))
        return out

    def masked(j, h, s, diagonal):
        return (jnp.where(keep, s, NEG) if diagonal else s), None

    _attend(t, 0, t, scores, masked, vt_ref, acc_ref, o_ref)


def _fox(p3d, extras, vt):
    b, s, _ = p3d.shape
    return pl.pallas_call(
        _fox_kernel,
        out_shape=jax.ShapeDtypeStruct((b, s, WIDTH), bf16),
        grid=(b, s // TQ),
        in_specs=[
            pl.BlockSpec((None, TQ, WIDTH), lambda i, t: (i, t, SEC_QC)),
            pl.BlockSpec((None, s, WIDTH), lambda i, t: (i, 0, SEC_KC)),
            pl.BlockSpec((None, s, WIDTH), lambda i, t: (i, 0, 0)),
            pl.BlockSpec((WIDTH, s), lambda i, t: (2, i)),
        ],
        out_specs=pl.BlockSpec((None, TQ, WIDTH), lambda i, t: (i, t, 0)),
        scratch_shapes=[pltpu.VMEM((N_HEADS, ACC_ROWS, TQ), f32)],
        compiler_params=pltpu.CompilerParams(
            dimension_semantics=("arbitrary", "arbitrary"), vmem_limit_bytes=VMEM_LIMIT_BYTES),
        name="fox",
    )(p3d, p3d, extras, vt)


def _moba_kernel(q_ref, k_ref, vt_ref, o_ref, acc_ref, km_ref, sel_ref, *, n_blk):
    t = pl.program_id(1)
    nb_pad = km_ref.shape[0]

    @pl.when(t == 0)
    def _():
        km_ref[...] = jnp.zeros_like(km_ref)
        for blk in range(n_blk):
            kb = k_ref[blk * MOBA_BLOCK:(blk + 1) * MOBA_BLOCK, :].astype(f32)
            km_ref[blk:blk + 1, :] = jnp.sum(kb, axis=0, keepdims=True) * (1.0 / MOBA_BLOCK)

    keep = _diag_keep()
    blk_row =lax.broadcasted_iota(jnp.int32, (nb_pad, TQ), 0)
    blk_row_f = blk_row.astype(f32)
    qt = _queries_t(q_ref)

    for h in range(N_HEADS):
        gate = jnp.dot(km_ref[:, _slab(h)].astype(bf16), qt[h], preferred_element_type=f32)
        g = jnp.where(blk_row < t, gate, NEG)
        sel = jnp.zeros((nb_pad, TQ), f32)
        for _ in range(min(MOBA_TOPK, n_blk - 1)):
            mx = jnp.max(g, axis=0, keepdims=True)
            first = jnp.min(jnp.where(g == mx, blk_row_f, float(nb_pad)), axis=0, keepdims=True)
            hit = blk_row_f == first
            sel = jnp.where(hit, jnp.where(mx > 0.5 * NEG, 1.0, sel), sel)
            g = jnp.where(hit, NEG, g)
        sel_ref[h] = sel

    def scores(j):
        off = pl.multiple_of(j * TK, TK)
        return [jnp.dot(k_ref[pl.ds(off, TK), _slab(h)], qt[h], preferred_element_type=f32)
                for h in range(N_HEADS)]

    def masked(j, h, s, diagonal):
        if diagonal:
            return jnp.where(keep, s, NEG), None
        return s, sel_ref[h, pl.ds(j, 1), :] > 0.5

    _attend(t, 0, t, scores, masked, vt_ref, acc_ref, o_ref)


def _moba(p3d, vt):
    b, s, _ = p3d.shape
    assert TQ == MOBA_BLOCK and TK == MOBA_BLOCK and s % MOBA_BLOCK == 0
    n_blk = s // MOBA_BLOCK
    nb_pad = -(-n_blk // BF16_ROWS) * BF16_ROWS
    return pl.pallas_call(
        functools.partial(_moba_kernel, n_blk=n_blk),
        out_shape=jax.ShapeDtypeStruct((b, s, WIDTH), bf16),
        grid=(b, n_blk),
        in_specs=[
            pl.BlockSpec((None, TQ, WIDTH), lambda i, t: (i, t, SEC_QA)),
            pl.BlockSpec((None, s, WIDTH), lambda i, t: (i, 0, SEC_KA)),
            pl.BlockSpec((WIDTH, s), lambda i, t: (0, i)),
        ],
        out_specs=pl.BlockSpec((None, TQ, WIDTH), lambda i, t: (i, t, 0)),
        scratch_shapes=[pltpu.VMEM((N_HEADS, ACC_ROWS, TQ), f32),
                        pltpu.VMEM((nb_pad, WIDTH), f32),
                        pltpu.VMEM((N_HEADS, nb_pad, TQ), f32)],
        compiler_params=pltpu.CompilerParams(
            dimension_semantics=("arbitrary", "arbitrary"), vmem_limit_bytes=VMEM_LIMIT_BYTES),
        name="moba",
    )(p3d, p3d, vt)


def _dsa_kernel(q_ref, k_ref, vt_ref, qi_ref, kk_ref, sm_ref, o_ref, acc_ref, sc_ref, sc16_ref, *, top, seq):
    t = pl.program_id(1)
    n_kv = t + 1
    n_chunks = lax.shift_right_logical(n_kv + 1, 1)
    n_keys = (n_kv * TK).astype(f32)
    k_top = float(top)
    keep = _diag_keep()
    chunk_row = lax.broadcasted_iota(jnp.int32, (SEARCH_CHUNK, TQ), 0)
    group_row = lax.broadcasted_iota(jnp.int32, (SUBLANE, TQ), 0)
    w_rows = sm_ref[...].T
    qit = _queries_t(qi_ref)

    @pl.when((n_kv & 1) == 1)
    def _():
        pad = pl.ds(pl.multiple_of(n_kv * TK, TK), TK)
        sc_ref[pad, :] = jnp.full((TK, TQ), -jnp.inf, f32)
        sc16_ref[pad, :] = jnp.full((TK, TQ), -jnp.inf, bf16)

    def score_body(j, _):
        off = pl.multiple_of(j * TK, TK)
        kkb = kk_ref[pl.ds(off, TK), :]
        rel = [jnp.dot(kkb, qit[hh], preferred_element_type=f32) for hh in range(IDX_HEADS)]
        isc = jnp.zeros((TK, TQ), f32)
        for hh in range(IDX_HEADS):
            isc = isc + w_rows[SMALL_WIDX + hh:SMALL_WIDX + hh + 1, :] * jnp.maximum(rel[hh], 0.0)
        isc = jnp.where((j < t) | keep, isc, -jnp.inf)
        sc_ref[pl.ds(off, TK), :] = isc
        sc16_ref[pl.ds(off, TK), :] = isc.astype(bf16)
        return 0

    lax.fori_loop(0, n_kv, score_body, 0)

    def count_in(ref, rows_per_vreg, one, pred):
        def add_rows(off, n_rows, accs):
            chunk = ref[pl.ds(off, n_rows), :]
            accs = list(accs)
            for g in range(n_rows // rows_per_vreg):
                rows = chunk[g * rows_per_vreg:(g + 1) * rows_per_vreg, :]
                hit = jnp.where(pred(rows, off + g * rows_per_vreg), one, one * 0)
                accs[g % COUNT_CHAINS] = accs[g % COUNT_CHAINS] + hit
            return tuple(accs)

        def body(c, accs):
            return add_rows(pl.multiple_of(c * SEARCH_CHUNK, SEARCH_CHUNK), SEARCH_CHUNK, accs)
        init = tuple(jnp.zeros((rows_per_vreg, TQ), one.dtype) for _ in range(COUNT_CHAINS))
        accs = lax.fori_loop(0, lax.shift_right_logical(n_kv, 1), body, init)
        accs = lax.cond((n_kv & 1) == 1,
                        lambda a: add_rows(pl.multiple_of((n_kv - 1) * TK, TK), TK, a), lambda a: a, accs)
        return jnp.sum(sum(a.astype(f32) for a in accs), axis=0, keepdims=True)

    def count(pred):
        return count_in(sc_ref, SUBLANE, jnp.float32(1), pred)

    def count16(pred):
        return count_in(sc16_ref, BF16_ROWS, jnp.bfloat16(1), pred)

    def as_score(code):
        return pltpu.bitcast(jnp.where(code < 0, code ^ 0x7FFFFFFF, code), f32)

    def as_score16(code16):
        return pltpu.bitcast(jnp.left_shift(jnp.where(code16 < 0, code16 ^ 0x7FFF, code16), 16), f32)

    def search(n_bits, first, value_of, count_ge, floor_count):
        def bit_body(i, carry):
            code, n_ge = carry
            cand = code + jnp.left_shift(jnp.int32(1), (n_bits - 1) - i)
            cand_f = value_of(cand)
            c = count_ge(cand_f)
            ok = (c >= k_top) & (cand_f > -jnp.inf)
            return jnp.where(ok, cand, code), jnp.where(ok, c, n_ge)
        return lax.fori_loop(0, n_bits, bit_body, (first, floor_count))

    i16_min = jnp.full((1, TQ), I16_MIN, jnp.int32)
    code16, _ = search(16, i16_min, as_score16, lambda v: count16(lambda r, _: r >= v.astype(bf16)),
                       jnp.broadcast_to(n_keys, (1, TQ)))
    has_thr = code16 > I16_MIN
    centre = code16 * 65536 + jnp.where(code16 < 0, 65535, 0)
    first = jnp.where(has_thr, centre - 65536, INT_MIN)
    n_first = jnp.where(has_thr, count(lambda sc, _: sc >= as_score(first)), n_keys)
    code, n_ge = search(17, first, as_score, lambda v: count(lambda sc, _: sc >= v), n_first)
    thr = jnp.where(has_thr, as_score(code), -jnp.inf)

    no_cut = jnp.where(has_thr, seq + 1, 0).astype(jnp.int32)
    over = jnp.max(jnp.where(has_thr & (n_ge > k_top), 1.0, 0.0))

    def tie_cut():
        need = k_top - count(lambda sc, off: sc > thr)
        n_bits = seq.bit_length()

        def cut_body(i, cut):
            cand = cut + jnp.left_shift(jnp.int32(1), (n_bits - 1) - i)
            c = count(lambda sc, first: (sc == thr) & (group_row + first < cand))
            return jnp.where(c <= need, cand, cut)

        cut = lax.fori_loop(0, n_bits, cut_body, jnp.zeros((1, TQ), jnp.int32))
        return jnp.where(has_thr, cut, 0)

    cut = lax.cond(over > 0.5, tie_cut, lambda: no_cut)

    def ceiling_body(c, _):
        off = pl.multiple_of(c * SEARCH_CHUNK, SEARCH_CHUNK)
        sc = sc_ref[pl.ds(off, SEARCH_CHUNK), :]
        chosen = (sc > thr) | ((sc == thr) & (chunk_row + off < cut))
        sc_ref[pl.ds(off, SEARCH_CHUNK), :] = jnp.where(chosen, -NEG, NEG).astype(f32)
        return 0

    lax.fori_loop(0, n_chunks, ceiling_body, 0)

    qt = _queries_t(q_ref)

    def scores(j):
        off = pl.multiple_of(j * TK, TK)
        return [jnp.dot(k_ref[pl.ds(off, TK), _slab(h)], qt[h], preferred_element_type=f32)
                for h in range(N_HEADS)]

    def masked(j, h, s, diagonal):
        ceiling = sc_ref[pl.ds(pl.multiple_of(j * TK, TK), TK), :]
        return jnp.minimum(s, ceiling), None

    _attend(0, 1, t, scores, masked, vt_ref, acc_ref, o_ref)


def _dsa(p3d, vt, kk, small3d):
    b, s, _ = p3d.shape
    top = min(DSA_TOPK_MAX, s // 4)
    return pl.pallas_call(
        functools.partial(_dsa_kernel, top=top, seq=s),
        out_shape=jax.ShapeDtypeStruct((b, s, WIDTH), bf16),
        grid=(b, s // TQ),
        in_specs=[
            pl.BlockSpec((None, TQ, WIDTH), lambda i, t: (i, t, SEC_QB)),
            pl.BlockSpec((None, s, WIDTH), lambda i, t: (i, 0, SEC_KB)),
            pl.BlockSpec((WIDTH, s), lambda i, t: (1, i)),
            pl.BlockSpec((None, TQ, WIDTH), lambda i, t: (i, t, SEC_QIDX)),
            pl.BlockSpec((None, s, LANE), lambda i, t: (i, 0, 0)),
            pl.BlockSpec((None, TQ, LANE), lambda i, t: (i, t, 0)),
        ],
        out_specs=pl.BlockSpec((None, TQ, WIDTH), lambda i, t: (i, t, 0)),
        scratch_shapes=[pltpu.VMEM((N_HEADS, ACC_ROWS, TQ), f32), pltpu.VMEM((s, TQ), f32), pltpu.VMEM((s, TQ), bf16)],
        compiler_params=pltpu.CompilerParams(
            dimension_semantics=("arbitrary", "arbitrary"), vmem_limit_bytes=VMEM_LIMIT_BYTES),
        name="dsa",
    )(p3d, p3d, vt, p3d, kk, small3d)


def _out_kernel(oa_ref, ob_ref, oc_ref, ga_ref, gb_ref, gc_ref, m0_ref, m1_ref, m2_ref,
                x_ref, wb_ref, wo_ref, fg_ref, y_ref, *, final_norm):
    merged = None
    for n, (o_ref, g_ref, mg_ref) in enumerate(((oa_ref, ga_ref, m0_ref), (ob_ref, gb_ref, m1_ref),
                                                (oc_ref, gc_ref, m2_ref))):
        gated = (o_ref[...].astype(f32) * g_ref[...].astype(f32)).astype(bf16)
        y = jnp.dot(gated, wb_ref[n], preferred_element_type=f32) * mg_ref[...].astype(f32)
        merged = y if merged is None else merged + y
    out = x_ref[...] + jnp.dot(merged.astype(bf16), wo_ref[...], preferred_element_type=f32)
    if final_norm:
        ms = jnp.mean(out * out, axis=-1, keepdims=True)
        out = out * lax.rsqrt(ms + RMS_EPS) * fg_ref[...]
    y_ref[...] = out


def _output(oa, ob, oc, p2d, x2d, w_branch, w_out, final_gain, final_norm):
    n_rows, d_model = x2d.shape
    tm = min(OUT_TM, n_rows)
    assert SEC_MERGE * WIDTH % d_model == 0
    mg0 = SEC_MERGE * WIDTH // d_model
    row = lambda c: (lambda i: (i, c))
    return pl.pallas_call(
        functools.partial(_out_kernel, final_norm=final_norm),
        out_shape=jax.ShapeDtypeStruct((n_rows, d_model), f32),
        grid=(n_rows // tm,),
        in_specs=[
            pl.BlockSpec((tm, WIDTH), row(0)), pl.BlockSpec((tm, WIDTH), row(0)), pl.BlockSpec((tm, WIDTH), row(0)),
            pl.BlockSpec((tm, WIDTH), row(SEC_GA)), pl.BlockSpec((tm, WIDTH), row(SEC_GB)),
            pl.BlockSpec((tm, WIDTH), row(SEC_GC)),
            pl.BlockSpec((tm, d_model), row(mg0)), pl.BlockSpec((tm, d_model), row(mg0 + 1)),
            pl.BlockSpec((tm, d_model), row(mg0 + 2)),
            pl.BlockSpec((tm, d_model), row(0)),
            pl.BlockSpec((N_BRANCH, WIDTH, d_model), lambda i: (0, 0, 0)),
            pl.BlockSpec((d_model, d_model), lambda i: (0, 0)),
            pl.BlockSpec((1, d_model), lambda i: (0, 0)),
        ],
        out_specs=pl.BlockSpec((tm, d_model), row(0)),
        compiler_params=pltpu.CompilerParams(
            dimension_semantics=("arbitrary",), vmem_limit_bytes=VMEM_LIMIT_BYTES),
        name="out",
    )(oa, ob, oc, p2d, p2d, p2d, p2d, p2d, p2d, x2d, w_branch, w_out, final_gain)


def _pack_w_in(w):
    sec = lambda o: w[:, o * WIDTH:(o + 1) * WIDTH]
    small0 = 12 * WIDTH + IDX_HEADS * IDX_DIM
    merge0 = small0 + IDX_DIM + IDX_HEADS + N_HEADS
    order = (0, 1, 4, 5, 12, 8, 9, 3, 7, 11)
    main = jnp.concatenate([sec(o) for o in order] + [w[:, merge0:]], axis=1)
    small = jnp.pad(w[:, small0:merge0], ((0, 0), (0, LANE - (merge0 - small0))))
    values_t = jnp.concatenate([sec(2), sec(6), sec(10)], axis=1).T
    return main.astype(bf16), small.astype(bf16), values_t.astype(bf16)


def _rope_tables(seq):
    half = ROPE_DIM // 2
    inv_freq = jnp.power(jnp.float32(ROPE_THETA), -jnp.arange(0, ROPE_DIM, 2, dtype=f32) / ROPE_DIM)
    ang = jnp.arange(seq).astype(f32)[:, None] * inv_freq[None, :]
    cos, sin = jnp.cos(ang), jnp.sin(ang)
    pad = jnp.zeros((seq, HEAD_DIM - ROPE_DIM), f32)
    zero = jnp.zeros((seq, half), f32)
    tile = lambda a: jnp.tile(a, (1, LANE // HEAD_DIM))
    cos_t = tile(jnp.concatenate([cos, cos, pad + 1.0], axis=1))
    sin_a = tile(jnp.concatenate([-sin, zero, pad], axis=1))
    sin_b = tile(jnp.concatenate([zero, sin, pad], axis=1))
    return cos_t, sin_a, sin_b


def kernel(x, norm_gain, w_in, forget_bias, w_branch, w_out, final_gain):
    b, s, d_model = x.shape
    depth = norm_gain.shape[0]
    assert s % TQ == 0 and s % min(PROJ_TM, s) == 0 and s % min(CSUM_CHUNK, s) == 0
    cos_t, sin_a, sin_b = _rope_tables(s)
    h2d = x.reshape(b * s, d_model)
    for layer in range(depth):
        w_main, w_small, w_vt = _pack_w_in(w_in[layer])
        gain = norm_gain[layer][None, :]
        fbias = jnp.zeros((1, LANE), f32).at[0, SMALL_F:SMALL_F + N_HEADS].set(forget_bias[layer])
        p2d, small2d = _project(h2d, gain, w_main, w_small, cos_t, sin_a, sin_b, fbias, s)
        vt = _project_values(h2d, gain, w_vt, s)
        p3d = p2d.reshape(b, s, N_SEC * WIDTH)
        small3d = small2d.reshape(b, s, LANE)
        extras = _cumsum_extras(small3d)
        k_idx = small3d[:, :, SMALL_KIDX:SMALL_KIDX + IDX_DIM].astype(bf16)
        kk = jnp.concatenate([k_idx, k_idx], axis=-1)
        oa = _moba(p3d, vt)
        ob = _dsa(p3d, vt, kk, small3d)
        oc = _fox(p3d, extras, vt)
        h2d = _output(oa.reshape(b * s, WIDTH), ob.reshape(b * s, WIDTH), oc.reshape(b * s, WIDTH),
                      p2d, h2d, w_branch[layer].astype(bf16), w_out[layer].astype(bf16),
                      final_gain[None, :], layer == depth - 1)
    return h2d.reshape(b, s, d_model)
```
